```python
import math
import jax, jax.numpy as jnp
from jax import lax
import numpy as np


D_MODEL = 1024
BATCH = 2
SEQ = 16384
DEPTH = 2

GROUP_WIDTH = D_MODEL // 4
LRU_HEADS = 4
LRU_HEAD_DIM = GROUP_WIDTH // LRU_HEADS
LRU_CONV = 4
LRU_C = 8.0
ATTN_HEADS = 4
ATTN_V_DIM = GROUP_WIDTH // ATTN_HEADS
ATTN_QK_DIM = ATTN_V_DIM // 2
ATTN_BLOCK = 128
CONF_KERNEL = 31
SC_KERNEL = 3
D_FF = 11 * D_MODEL // 4
FFN_KERNEL = 3
NORM_EPS = 1e-6
NEG_INF = -1e30
SPLIT_SIZES = (GROUP_WIDTH,) * 5 + (2 * GROUP_WIDTH,) + (GROUP_WIDTH,) * 3
D_PROJ = 10 * GROUP_WIDTH

kernel_name = 'hybrid_rglru_diffattn_conformer_shortconv_convffn'


def rms_norm(x, g):
    xf = x.astype(jnp.float32)
    y = xf * lax.rsqrt(jnp.mean(xf * xf, axis=-1, keepdims=True) + NORM_EPS)
    return (y * g.astype(jnp.float32)).astype(x.dtype)


def layer_norm(x, g, b):
    xf = x.astype(jnp.float32)
    mu = jnp.mean(xf, axis=-1, keepdims=True)
    xc = xf - mu
    var = jnp.mean(xc * xc, axis=-1, keepdims=True)
    y = xc * lax.rsqrt(var + 1e-5) * g.astype(jnp.float32) + b.astype(jnp.float32)
    return y.astype(x.dtype)


def causal_dwconv(x, w, b=None):
    K, C = w.shape
    y = lax.conv_general_dilated(
        x, w[:, None, :].astype(x.dtype), window_strides=(1,), padding=[(K - 1, 0)],
        dimension_numbers=('NWC', 'WIO', 'NWC'), feature_group_count=C)
    if b is not None:
        y = y + b.astype(y.dtype)
    return y


def _lin_rec_combine(left, right):
    a1, b1 = left
    a2, b2 = right
    return a1 * a2, a2 * b1 + b2


def rg_lru(x, w_a, b_a, w_x, b_x, lam):
    Bsz, S, W = x.shape
    xh = x.reshape(Bsz, S, LRU_HEADS, LRU_HEAD_DIM)
    r = jax.nn.sigmoid(jnp.einsum('bshi,hij->bshj', xh, w_a).reshape(Bsz, S, W) + b_a)
    i = jax.nn.sigmoid(jnp.einsum('bshi,hij->bshj', xh, w_x).reshape(Bsz, S, W) + b_x)
    log_a = -LRU_C * r.astype(jnp.float32) * jax.nn.softplus(-lam.astype(jnp.float32))
    a = jnp.exp(log_a)
    u = jnp.sqrt(-jnp.expm1(2.0 * log_a)) * (i * x).astype(jnp.float32)
    _, h = lax.associative_scan(_lin_rec_combine, (a, u), axis=1)
    return h.astype(x.dtype)


def diff_attention(q, k, v, lam_q1, lam_k1, lam_q2, lam_k2, subln_g, lambda_init):
    Bsz, S, _ = q.shape
    nb = S // ATTN_BLOCK

    def split_qk(t):
        t = t.reshape(Bsz, nb, ATTN_BLOCK, ATTN_HEADS, 2, ATTN_QK_DIM)
        return jnp.transpose(t, (1, 4, 0, 3, 2, 5)).astype(jnp.float32)

    qb = split_qk(q) * (ATTN_QK_DIM ** -0.5)
    kb = split_qk(k)
    vb = jnp.transpose(v.reshape(Bsz, nb, ATTN_BLOCK, ATTN_HEADS, ATTN_V_DIM),
                       (1, 0, 3, 2, 4)).astype(jnp.float32)
    lam = (jnp.exp(jnp.sum(lam_q1.astype(jnp.float32) * lam_k1.astype(jnp.float32)))
           - jnp.exp(jnp.sum(lam_q2.astype(jnp.float32) * lam_k2.astype(jnp.float32)))
           + lambda_init)
    pos = jnp.arange(ATTN_BLOCK)

    def q_block(args):
        qi, qblk = args
        q_pos = qi * ATTN_BLOCK + pos

        def body(j, carry):
            m, l, acc = carry
            kj = lax.dynamic_index_in_dim(kb, j, 0, keepdims=False)
            vj = lax.dynamic_index_in_dim(vb, j, 0, keepdims=False)
            s = jnp.einsum('nbhqd,nbhkd->nbhqk', qblk, kj)
            causal = q_pos[:, None] >= (j * ATTN_BLOCK + pos)[None, :]
            s = jnp.where(causal, s, NEG_INF)
            m_new = jnp.maximum(m, jnp.max(s, axis=-1))
            corr = jnp.exp(m - m_new)
            p = jnp.exp(s - m_new[..., None])
            l = l * corr + jnp.sum(p, axis=-1)
            acc = acc * corr[..., None] + jnp.einsum('nbhqk,bhkd->nbhqd', p, vj)
            return m_new, l, acc

        init = (jnp.full((2, Bsz, ATTN_HEADS, ATTN_BLOCK), NEG_INF, jnp.float32),
                jnp.zeros((2, Bsz, ATTN_HEADS, ATTN_BLOCK), jnp.float32),
                jnp.zeros((2, Bsz, ATTN_HEADS, ATTN_BLOCK, ATTN_V_DIM), jnp.float32))
        _, l, acc = lax.fori_loop(0, qi + 1, body, init)
        o = acc / l[..., None]
        return o[0] - lam * o[1]

    out = lax.map(q_block, (jnp.arange(nb), qb))
    out = rms_norm(out, subln_g) * (1.0 - lambda_init)
    out = jnp.transpose(out, (1, 0, 3, 2, 4)).reshape(Bsz, S, ATTN_HEADS * ATTN_V_DIM)
    return out.astype(v.dtype)


def hybrid_mixer(h, w_in, lru_conv_w, lru_conv_b, lru_wa, lru_ba, lru_wx, lru_bx, lru_lambda,
                 attn_lq1, attn_lk1, attn_lq2, attn_lk2, attn_subln,
                 conf_dw_w, conf_dw_b, conf_ln_g, conf_ln_b, sc_conv_w, w_out, lambda_init):
    proj = h @ w_in
    idx = np.cumsum(SPLIT_SIZES)[:-1].tolist()
    ax, ag, bq, bk, bv, cglu, dx, db, dc = jnp.split(proj, idx, axis=-1)
    ax = causal_dwconv(ax, lru_conv_w, lru_conv_b)
    ya = rg_lru(ax, lru_wa, lru_ba, lru_wx, lru_bx, lru_lambda) * jax.nn.gelu(ag, approximate=True)
    yb = diff_attention(bq, bk, bv, attn_lq1, attn_lk1, attn_lq2, attn_lk2, attn_subln, lambda_init)
    c_lin, c_gate = jnp.split(cglu, 2, axis=-1)
    c = c_lin * jax.nn.sigmoid(c_gate)
    c = causal_dwconv(c, conf_dw_w, conf_dw_b)
    yc = jax.nn.silu(layer_norm(c, conf_ln_g, conf_ln_b))
    yd = db * causal_dwconv(dc * dx, sc_conv_w)
    return jnp.concatenate([ya, yb, yc, yd], axis=-1) @ w_out


def conv_ffn(h, w_up, conv_w, conv_b, w_down):
    u, g = jnp.split(h @ w_up, [D_FF], axis=-1)
    g = causal_dwconv(g, conv_w, conv_b)
    return (jax.nn.gelu(g, approximate=True) * u) @ w_down


def setup_inputs(seed: int = 0) -> dict:
    key = jax.random.key(seed)
    ks = iter(jax.random.split(key, 40))
    L = DEPTH
    G = GROUP_WIDTH

    def nrm(shape, scale):
        return scale * jax.random.normal(next(ks), shape, jnp.float32)

    def gain(n):
        return 1.0 + nrm((L, n), 0.02)

    x = jax.random.normal(next(ks), (BATCH, SEQ, D_MODEL), jnp.float32)
    u = jax.random.uniform(next(ks), (L, G), jnp.float32, minval=0.9, maxval=0.999)
    s = u ** (1.0 / LRU_C)
    lru_lambda = jnp.log(s) - jnp.log1p(-s)
    return {
        'x': x,
        'norm_mix_pre': gain(D_MODEL),
        'norm_mix_post': gain(D_MODEL),
        'norm_ffn_pre': gain(D_MODEL),
        'norm_ffn_post': gain(D_MODEL),
        'w_in': nrm((L, D_MODEL, D_PROJ), D_MODEL ** -0.5),
        'lru_conv_w': nrm((L, LRU_CONV, G), LRU_CONV ** -0.5),
        'lru_conv_b': nrm((L, G), 0.02),
        'lru_wa': nrm((L, LRU_HEADS, LRU_HEAD_DIM, LRU_HEAD_DIM), LRU_HEAD_DIM ** -0.5),
        'lru_ba': nrm((L, G), 0.02),
        'lru_wx': nrm((L, LRU_HEADS, LRU_HEAD_DIM, LRU_HEAD_DIM), LRU_HEAD_DIM ** -0.5),
        'lru_bx': nrm((L, G), 0.02),
        'lru_lambda': lru_lambda,
        'attn_lq1': nrm((L, ATTN_QK_DIM), 0.1),
        'attn_lk1': nrm((L, ATTN_QK_DIM), 0.1),
        'attn_lq2': nrm((L, ATTN_QK_DIM), 0.1),
        'attn_lk2': nrm((L, ATTN_QK_DIM), 0.1),
        'attn_subln': gain(ATTN_V_DIM),
        'conf_dw_w': nrm((L, CONF_KERNEL, G), CONF_KERNEL ** -0.5),
        'conf_dw_b': nrm((L, G), 0.02),
        'conf_ln_g': gain(G),
        'conf_ln_b': nrm((L, G), 0.02),
        'sc_conv_w': nrm((L, SC_KERNEL, G), SC_KERNEL ** -0.5),
        'w_out': nrm((L, D_MODEL, D_MODEL), D_MODEL ** -0.5),
        'ffn_w_up': nrm((L, D_MODEL, 2 * D_FF), D_MODEL ** -0.5),
        'ffn_conv_w': nrm((L, FFN_KERNEL, D_FF), FFN_KERNEL ** -0.5),
        'ffn_conv_b': nrm((L, D_FF), 0.02),
        'ffn_w_down': nrm((L, D_FF, D_MODEL), D_FF ** -0.5),
    }


def reference(x, norm_mix_pre, norm_mix_post, norm_ffn_pre, norm_ffn_post, w_in,
              lru_conv_w, lru_conv_b, lru_wa, lru_ba, lru_wx, lru_bx, lru_lambda,
              attn_lq1, attn_lk1, attn_lq2, attn_lk2, attn_subln,
              conf_dw_w, conf_dw_b, conf_ln_g, conf_ln_b, sc_conv_w, w_out,
              ffn_w_up, ffn_conv_w, ffn_conv_b, ffn_w_down):
    h = x
    for l in range(DEPTH):
        lambda_init = 0.8 - 0.6 * math.exp(-0.3 * l)
        m = hybrid_mixer(rms_norm(h, norm_mix_pre[l]), w_in[l],
                         lru_conv_w[l], lru_conv_b[l], lru_wa[l], lru_ba[l], lru_wx[l], lru_bx[l],
                         lru_lambda[l], attn_lq1[l], attn_lk1[l], attn_lq2[l], attn_lk2[l],
                         attn_subln[l], conf_dw_w[l], conf_dw_b[l], conf_ln_g[l], conf_ln_b[l],
                         sc_conv_w[l], w_out[l], lambda_init)
        h = h + rms_norm(m, norm_mix_post[l])
        f = conv_ffn(rms_norm(h, norm_ffn_pre[l]), ffn_w_up[l], ffn_conv_w[l], ffn_conv_b[l],
                     ffn_w_down[l])
        h = h + rms_norm(f, norm_ffn_post[l])
    return h
```

```python
import functools
import math

import jax
import jax.numpy as jnp
from jax import lax
from jax.experimental import pallas as pl
from jax.experimental.pallas import tpu as pltpu

F32 = jnp.float32
BF16 = jnp.bfloat16

D_MODEL = 1024
GROUP = D_MODEL // 4
LRU_HEADS = 4
LRU_HEAD_DIM = GROUP // LRU_HEADS
LRU_CONV = 4
LRU_C = 8.0
ATTN_HEADS = 4
ATTN_V_DIM = GROUP // ATTN_HEADS
ATTN_QK_DIM = ATTN_V_DIM // 2
CONF_KERNEL = 31
SC_KERNEL = 3
D_FF = 11 * D_MODEL // 4
FFN_KERNEL = 3
NORM_EPS = 1e-6
LN_EPS = 1e-5
NEG_INF = -1e30

N_COMBO = 2 * ATTN_HEADS
V_ROWS = ATTN_V_DIM + 16
REST_W = 7 * GROUP
LOG2E = 1.4426950408889634
VMEM_LIMIT_BYTES = 56 * 1024 * 1024

NT_DIMS = (((1,), (1,)), ((), ()))
TN_DIMS = (((0,), (0,)), ((), ()))


def _rms(x, g):
    return x * lax.rsqrt(jnp.mean(x * x, axis=-1, keepdims=True) + NORM_EPS) * g


def _const_spec(shape):
    nd = len(shape)
    return pl.BlockSpec(shape, lambda b, t: (0,) * nd, pipeline_mode=pl.Buffered(1))


def _params(sem):
    return pltpu.CompilerParams(dimension_semantics=sem, vmem_limit_bytes=VMEM_LIMIT_BYTES)


def _inproj_kernel(h_ref, g_ref, wrest_ref, wk_ref, wqt_ref, wvt_ref,
                   rest_ref, k_ref, qt_ref, vt_ref, *, tm):
    yb = _rms(h_ref[0], g_ref[...]).astype(BF16)
    rest_ref[0] = jnp.dot(yb, wrest_ref[...], preferred_element_type=F32)
    k_ref[0] = jnp.dot(yb, wk_ref[...], preferred_element_type=F32).astype(BF16)
    qt = lax.dot_general(wqt_ref[...], yb, NT_DIMS, preferred_element_type=F32)
    qt_ref[0] = (qt * (ATTN_QK_DIM ** -0.5 * LOG2E)).astype(BF16)
    vt = lax.dot_general(wvt_ref[...], yb, NT_DIMS, preferred_element_type=F32).astype(BF16)
    ones = jnp.ones((V_ROWS - ATTN_V_DIM, tm), BF16)
    for h in range(ATTN_HEADS):
        vt_ref[0, h, 0:ATTN_V_DIM, :] = vt[h * ATTN_V_DIM:(h + 1) * ATTN_V_DIM]
        vt_ref[0, h, ATTN_V_DIM:V_ROWS, :] = ones


def _inproj(h, g, wrest, wk, wqt, wvt, *, tm):
    B, S, D = h.shape
    return pl.pallas_call(
        functools.partial(_inproj_kernel, tm=tm),
        grid=(B, S // tm),
        in_specs=[
            pl.BlockSpec((1, tm, D), lambda b, t: (b, t, 0)),
            _const_spec((1, D)),
            _const_spec((D, REST_W)),
            _const_spec((D, GROUP)),
            _const_spec((GROUP, D)),
            _const_spec((GROUP, D)),
        ],
        out_specs=[
            pl.BlockSpec((1, tm, REST_W), lambda b, t: (b, t, 0)),
            pl.BlockSpec((1, tm, GROUP), lambda b, t: (b, t, 0)),
            pl.BlockSpec((1, GROUP, tm), lambda b, t: (b, 0, t)),
            pl.BlockSpec((1, ATTN_HEADS, V_ROWS, tm), lambda b, t: (b, 0, 0, t)),
        ],
        out_shape=[
            jax.ShapeDtypeStruct((B, S, REST_W), F32),
            jax.ShapeDtypeStruct((B, S, GROUP), BF16),
            jax.ShapeDtypeStruct((B, GROUP, S), BF16),
            jax.ShapeDtypeStruct((B, ATTN_HEADS, V_ROWS, S), BF16),
        ],
        compiler_params=_params(("arbitrary", "arbitrary")),
        name="inproj",
    )(h, g, wrest, wk, wqt, wvt)


def _attn_kernel(lq1_ref, lk1_ref, lq2_ref, lk2_ref, gsub_ref, qt_ref, k_ref, vt_ref,
                 o_ref, wq_ref, m_ref, acc_ref, *, blk, lambda_init):
    i = pl.program_id(1)

    @pl.when(i == 0)
    def _():
        wq_ref[...] = jnp.zeros(wq_ref.shape, BF16)

    for c in range(N_COMBO):
        rows = slice(c * ATTN_QK_DIM, (c + 1) * ATTN_QK_DIM)
        wq_ref[c, rows, :] = qt_ref[0, rows, :]
    m_ref[...] = jnp.full(m_ref.shape, NEG_INF, F32)
    acc_ref[...] = jnp.zeros(acc_ref.shape, F32)

    def step(j, diagonal):
        off = pl.multiple_of(j * blk, blk)
        kb = k_ref[0, pl.ds(off, blk), :]
        if diagonal:
            key_pos = lax.broadcasted_iota(jnp.int32, (blk, blk), 0)
            qry_pos = lax.broadcasted_iota(jnp.int32, (blk, blk), 1)
            causal = key_pos <= qry_pos
        for h in range(ATTN_HEADS):
            vt = vt_ref[0, h, :, pl.ds(off, blk)]
            for n in range(2):
                c = 2 * h + n
                s = jnp.dot(kb, wq_ref[c], preferred_element_type=F32)
                if diagonal:
                    s = jnp.where(causal, s, NEG_INF)
                m_old = m_ref[c]
                m_new = jnp.maximum(m_old, jnp.max(s, axis=0, keepdims=True))
                p = jnp.exp2(s - m_new)
                corr = jnp.exp2(m_old - m_new)
                pv = jnp.dot(vt, p.astype(BF16), preferred_element_type=F32)
                acc_ref[c] = acc_ref[c] * corr + pv
                m_ref[c] = m_new

    def body(j, carry):
        step(j, False)
        return carry

    lax.fori_loop(0, i, body, 0)
    step(i, True)

    lam = (jnp.exp(jnp.sum(lq1_ref[...] * lk1_ref[...], keepdims=True))
           - jnp.exp(jnp.sum(lq2_ref[...] * lk2_ref[...], keepdims=True))
           + lambda_init)
    for h in range(ATTN_HEADS):
        a0 = acc_ref[2 * h]
        a1 = acc_ref[2 * h + 1]
        o0 = a0[0:ATTN_V_DIM] / a0[ATTN_V_DIM:ATTN_V_DIM + 1]
        o1 = a1[0:ATTN_V_DIM] / a1[ATTN_V_DIM:ATTN_V_DIM + 1]
        o = o0 - lam * o1
        ms = jnp.mean(o * o, axis=0, keepdims=True)
        y = o * lax.rsqrt(ms + NORM_EPS) * gsub_ref[...]
        o_ref[0, h * ATTN_V_DIM:(h + 1) * ATTN_V_DIM, :] = y.astype(BF16)


def _attention(lq1, lk1, lq2, lk2, gsub, qt, k, vt, *, blk, lambda_init):
    B, S, _ = k.shape
    small = lambda shape: pl.BlockSpec(shape, lambda b, t: (0,) * len(shape))
    return pl.pallas_call(
        functools.partial(_attn_kernel, blk=blk, lambda_init=lambda_init),
        grid=(B, S // blk),
        in_specs=[
            small((1, ATTN_QK_DIM)), small((1, ATTN_QK_DIM)),
            small((1, ATTN_QK_DIM)), small((1, ATTN_QK_DIM)),
            small((ATTN_V_DIM, blk)),
            pl.BlockSpec((1, GROUP, blk), lambda b, t: (b, 0, t)),
            pl.BlockSpec((1, S, GROUP), lambda b, t: (b, 0, 0), pipeline_mode=pl.Buffered(1)),
            pl.BlockSpec((1, ATTN_HEADS, V_ROWS, S), lambda b, t: (b, 0, 0, 0),
                         pipeline_mode=pl.Buffered(1)),
        ],
        out_specs=pl.BlockSpec((1, GROUP, blk), lambda b, t: (b, 0, t)),
        out_shape=jax.ShapeDtypeStruct((B, GROUP, S), BF16),
        scratch_shapes=[
            pltpu.VMEM((N_COMBO, GROUP, blk), BF16),
            pltpu.VMEM((N_COMBO, 1, blk), F32),
            pltpu.VMEM((N_COMBO, V_ROWS, blk), F32),
        ],
        compiler_params=_params(("arbitrary", "arbitrary")),
        name="attn",
    )(lq1, lk1, lq2, lk2, gsub, qt, k, vt)


def _linear_scan(a, u):
    rows = a.shape[0]
    row = lax.broadcasted_iota(jnp.int32, (rows, 1), 0)
    d = 1
    while d < rows:
        keep = row >= d
        a_prev = pltpu.roll(a, d, 0)
        u_prev = pltpu.roll(u, d, 0)
        u = jnp.where(keep, a * u_prev + u, u)
        a = jnp.where(keep, a * a_prev, a)
        d *= 2
    return a, u


def _causal_taps(ext_ref, cur, w_ref, halo, taps, tm):
    ext_ref[halo:halo + tm, :] = cur
    acc = None
    for k in range(taps):
        term = w_ref[k:k + 1, :] * ext_ref[pl.ds(halo - (taps - 1) + k, tm), :]
        acc = term if acc is None else acc + term
    ext_ref[0:halo, :] = cur[tm - halo:tm]
    return acc


def _mixer_kernel(h_ref, rest_ref, ybt_ref, lcw_ref, lcb_ref, wgate_ref, bgate_ref, lam_ref,
                  cw_ref, cb_ref, lng_ref, lnb_ref, scw_ref, wout_ref, gpost_ref,
                  out_ref, axext_ref, cext_ref, dext_ref, hstate_ref, *, tm):
    G = GROUP

    @pl.when(pl.program_id(1) == 0)
    def _():
        axext_ref[...] = jnp.zeros(axext_ref.shape, F32)
        cext_ref[...] = jnp.zeros(cext_ref.shape, F32)
        dext_ref[...] = jnp.zeros(dext_ref.shape, F32)
        hstate_ref[...] = jnp.zeros(hstate_ref.shape, F32)

    ax = rest_ref[0, :, 0:G]
    xa = _causal_taps(axext_ref, ax, lcw_ref, 8, LRU_CONV, tm) + lcb_ref[...]
    gates = jnp.dot(xa.astype(BF16), wgate_ref[...], preferred_element_type=F32) + bgate_ref[...]
    r = jax.nn.sigmoid(gates[:, 0:G])
    ig = jax.nn.sigmoid(gates[:, G:2 * G])
    neg_lam = -lam_ref[...]
    softplus = jnp.maximum(neg_lam, 0.0) + jnp.log1p(jnp.exp(-jnp.abs(neg_lam)))
    log_a = (-LRU_C) * r * softplus
    a = jnp.exp(log_a)
    u = jnp.sqrt(1.0 - a * a) * (ig * xa)
    a_cum, h_loc = _linear_scan(a, u)
    hh = h_loc + a_cum * hstate_ref[0:1, :]
    hstate_ref[0:1, :] = hh[tm - 1:tm]
    ya = hh * jax.nn.gelu(rest_ref[0, :, G:2 * G], approximate=True)

    c = rest_ref[0, :, 2 * G:3 * G] * jax.nn.sigmoid(rest_ref[0, :, 3 * G:4 * G])
    cc = _causal_taps(cext_ref, c, cw_ref, 32, CONF_KERNEL, tm) + cb_ref[...]
    mu = jnp.mean(cc, axis=-1, keepdims=True)
    xc = cc - mu
    var = jnp.mean(xc * xc, axis=-1, keepdims=True)
    ln = xc * lax.rsqrt(var + LN_EPS) * lng_ref[...] + lnb_ref[...]
    yc = ln * jax.nn.sigmoid(ln)

    dd = rest_ref[0, :, 6 * G:7 * G] * rest_ref[0, :, 4 * G:5 * G]
    yd = rest_ref[0, :, 5 * G:6 * G] * _causal_taps(dext_ref, dd, scw_ref, 8, SC_KERNEL, tm)

    m = jnp.dot(ya.astype(BF16), wout_ref[0:G, :], preferred_element_type=F32)
    m += lax.dot_general(ybt_ref[0], wout_ref[G:2 * G, :], TN_DIMS, preferred_element_type=F32)
    m += jnp.dot(yc.astype(BF16), wout_ref[2 * G:3 * G, :], preferred_element_type=F32)
    m += jnp.dot(yd.astype(BF16), wout_ref[3 * G:4 * G, :], preferred_element_type=F32)
    out_ref[0] = h_ref[0] + _rms(m, gpost_ref[...])


def _mixer(h, rest, ybt, lcw, lcb, wgate, bgate, lam, cw, cb, lng, lnb, scw, wout, gpost, *, tm):
    B, S, D = h.shape
    G = GROUP
    return pl.pallas_call(
        functools.partial(_mixer_kernel, tm=tm),
        grid=(B, S // tm),
        in_specs=[
            pl.BlockSpec((1, tm, D), lambda b, t: (b, t, 0)),
            pl.BlockSpec((1, tm, REST_W), lambda b, t: (b, t, 0)),
            pl.BlockSpec((1, G, tm), lambda b, t: (b, 0, t)),
            _const_spec((LRU_CONV, G)), _const_spec((1, G)),
            _const_spec((G, 2 * G)), _const_spec((1, 2 * G)), _const_spec((1, G)),
            _const_spec((CONF_KERNEL, G)), _const_spec((1, G)),
            _const_spec((1, G)), _const_spec((1, G)),
            _const_spec((SC_KERNEL, G)),
            _const_spec((D, D)), _const_spec((1, D)),
        ],
        out_specs=pl.BlockSpec((1, tm, D), lambda b, t: (b, t, 0)),
        out_shape=jax.ShapeDtypeStruct((B, S, D), F32),
        scratch_shapes=[
            pltpu.VMEM((8 + tm, G), F32),
            pltpu.VMEM((32 + tm, G), F32),
            pltpu.VMEM((8 + tm, G), F32),
            pltpu.VMEM((8, G), F32),
        ],
        compiler_params=_params(("arbitrary", "arbitrary")),
        name="mixer",
    )(h, rest, ybt, lcw, lcb, wgate, bgate, lam, cw, cb, lng, lnb, scw, wout, gpost)


def _ffn_kernel(h_ref, gpre_ref, wup_ref, cw_ref, cb_ref, wdown_ref, gpost_ref,
                out_ref, gext_ref, *, tm, chunk):
    @pl.when(pl.program_id(1) == 0)
    def _():
        gext_ref[...] = jnp.zeros(gext_ref.shape, F32)

    x = h_ref[0]
    yb = _rms(x, gpre_ref[...]).astype(BF16)
    f = jnp.zeros((tm, D_MODEL), F32)
    for c in range(D_FF // chunk):
        lo = c * chunk
        u = jnp.dot(yb, wup_ref[:, lo:lo + chunk], preferred_element_type=F32)
        g = jnp.dot(yb, wup_ref[:, D_FF + lo:D_FF + lo + chunk], preferred_element_type=F32)
        ext = gext_ref.at[c]
        gc = _causal_taps(ext, g, cw_ref.at[:, lo:lo + chunk], 8, FFN_KERNEL, tm)
        gc = gc + cb_ref[:, lo:lo + chunk]
        act = (jax.nn.gelu(gc, approximate=True) * u).astype(BF16)
        f += jnp.dot(act, wdown_ref[lo:lo + chunk, :], preferred_element_type=F32)
    out_ref[0] = x + _rms(f, gpost_ref[...])


def _ffn(h, gpre, wup, cw, cb, wdown, gpost, *, tm, chunk):
    B, S, D = h.shape
    return pl.pallas_call(
        functools.partial(_ffn_kernel, tm=tm, chunk=chunk),
        grid=(B, S // tm),
        in_specs=[
            pl.BlockSpec((1, tm, D), lambda b, t: (b, t, 0)),
            _const_spec((1, D)),
            _const_spec((D, 2 * D_FF)),
            _const_spec((FFN_KERNEL, D_FF)), _const_spec((1, D_FF)),
            _const_spec((D_FF, D)), _const_spec((1, D)),
        ],
        out_specs=pl.BlockSpec((1, tm, D), lambda b, t: (b, t, 0)),
        out_shape=jax.ShapeDtypeStruct((B, S, D), F32),
        scratch_shapes=[pltpu.VMEM((D_FF // chunk, 8 + tm, chunk), F32)],
        compiler_params=_params(("arbitrary", "arbitrary")),
        name="ffn",
    )(h, gpre, wup, cw, cb, wdown, gpost)


def _block_diag(w):
    heads, hd, _ = w.shape
    eye = jnp.eye(heads, dtype=w.dtype)
    return (eye[:, None, :, None] * w[:, :, None, :]).reshape(heads * hd, heads * hd)


def _pick_tile(S, want):
    t = min(want, S)
    while S % t:
        t //= 2
    return t


def kernel(x, norm_mix_pre, norm_mix_post, norm_ffn_pre, norm_ffn_post, w_in, lru_conv_w, lru_conv_b, lru_wa, lru_ba, lru_wx, lru_bx, lru_lambda, attn_lq1, attn_lk1, attn_lq2, attn_lk2, attn_subln, conf_dw_w, conf_dw_b, conf_ln_g, conf_ln_b, sc_conv_w, w_out, ffn_w_up, ffn_conv_w, ffn_conv_b, ffn_w_down):
    B, S, D = x.shape
    assert D == D_MODEL
    depth = w_in.shape[0]
    G = GROUP
    tm = _pick_tile(S, 512)
    blk = _pick_tile(S, 256)
    assert tm >= 32 and blk % 128 == 0
    row = lambda v: v.reshape(1, -1)

    h = x
    for l in range(depth):
        lambda_init = 0.8 - 0.6 * math.exp(-0.3 * l)
        wi = w_in[l]
        wrest = jnp.concatenate([wi[:, 0:2 * G], wi[:, 5 * G:10 * G]], axis=1).astype(BF16)
        wk = wi[:, 3 * G:4 * G].astype(BF16)
        wqt = wi[:, 2 * G:3 * G].T.astype(BF16)
        wvt = wi[:, 4 * G:5 * G].T.astype(BF16)
        rest, k, qt, vt = _inproj(h, row(norm_mix_pre[l]), wrest, wk, wqt, wvt, tm=tm)

        gsub = jnp.broadcast_to((attn_subln[l] * (1.0 - lambda_init))[:, None], (ATTN_V_DIM, blk))
        ybt = _attention(row(attn_lq1[l]), row(attn_lk1[l]), row(attn_lq2[l]), row(attn_lk2[l]),
                         gsub, qt, k, vt, blk=blk, lambda_init=lambda_init)

        wgate = jnp.concatenate([_block_diag(lru_wa[l]), _block_diag(lru_wx[l])], axis=1).astype(BF16)
        bgate = jnp.concatenate([lru_ba[l], lru_bx[l]]).reshape(1, 2 * G)
        h = _mixer(h, rest, ybt, lru_conv_w[l], row(lru_conv_b[l]), wgate, bgate, row(lru_lambda[l]),
                   conf_dw_w[l], row(conf_dw_b[l]), row(conf_ln_g[l]), row(conf_ln_b[l]),
                   sc_conv_w[l], w_out[l].astype(BF16), row(norm_mix_post[l]), tm=tm)

        h = _ffn(h, row(norm_ffn_pre[l]), ffn_w_up[l].astype(BF16), ffn_conv_w[l],
                 row(ffn_conv_b[l]), ffn_w_down[l].astype(BF16), row(norm_ffn_post[l]),
                 tm=tm, chunk=256)
    return h
```

```python
import functools
import math

import jax
import jax.numpy as jnp
from jax import lax
from jax.experimental import pallas as pl
from jax.experimental.pallas import tpu as pltpu

F32 = jnp.float32
BF16 = jnp.bfloat16

D_MODEL = 1024
GROUP = D_MODEL // 4
LRU_HEADS = 4
LRU_HEAD_DIM = GROUP // LRU_HEADS
LRU_CONV = 4
LRU_C = 8.0
ATTN_HEADS = 4
ATTN_V_DIM = GROUP // ATTN_HEADS
ATTN_QK_DIM = ATTN_V_DIM // 2
CONF_KERNEL = 31
SC_KERNEL = 3
D_FF = 11 * D_MODEL // 4
FFN_KERNEL = 3
NORM_EPS = 1e-6
LN_EPS = 1e-5
NEG_INF = -1e30

N_COMBO = 2 * ATTN_HEADS
AHEAD = 5
V_ROWS = ATTN_V_DIM + 16
REST_W = 7 * GROUP
LOG2E = 1.4426950408889634
VMEM_LIMIT_BYTES = 56 * 1024 * 1024

NT_DIMS = (((1,), (1,)), ((), ()))
TN_DIMS = (((0,), (0,)), ((), ()))


def _rms(x, g):
    return x * lax.rsqrt(jnp.mean(x * x, axis=-1, keepdims=True) + NORM_EPS) * g


def _const_spec(shape):
    nd = len(shape)
    return pl.BlockSpec(shape, lambda b, t: (0,) * nd, pipeline_mode=pl.Buffered(1))


def _params(sem):
    return pltpu.CompilerParams(dimension_semantics=sem, vmem_limit_bytes=VMEM_LIMIT_BYTES)


def _inproj_kernel(h_ref, g_ref, wrest_ref, wk_ref, wqt_ref, wvt_ref,
                   rest_ref, k_ref, qt_ref, vt_ref, *, tm):
    yb = _rms(h_ref[0], g_ref[...]).astype(BF16)
    rest_ref[0] = jnp.dot(yb, wrest_ref[...], preferred_element_type=F32)
    k_ref[0] = jnp.dot(yb, wk_ref[...], preferred_element_type=F32).astype(BF16)
    qt = lax.dot_general(wqt_ref[...], yb, NT_DIMS, preferred_element_type=F32)
    qt_ref[0] = (qt * (ATTN_QK_DIM ** -0.5 * LOG2E)).astype(BF16)
    vt = lax.dot_general(wvt_ref[...], yb, NT_DIMS, preferred_element_type=F32).astype(BF16)
    ones = jnp.ones((V_ROWS - ATTN_V_DIM, tm), BF16)
    for h in range(ATTN_HEADS):
        vt_ref[0, h, 0:ATTN_V_DIM, :] = vt[h * ATTN_V_DIM:(h + 1) * ATTN_V_DIM]
        vt_ref[0, h, ATTN_V_DIM:V_ROWS, :] = ones


def _inproj(h, g, wrest, wk, wqt, wvt, *, tm):
    B, S, D = h.shape
    return pl.pallas_call(
        functools.partial(_inproj_kernel, tm=tm),
        grid=(B, S // tm),
        in_specs=[
            pl.BlockSpec((1, tm, D), lambda b, t: (b, t, 0)),
            _const_spec((1, D)),
            _const_spec((D, REST_W)),
            _const_spec((D, GROUP)),
            _const_spec((GROUP, D)),
            _const_spec((GROUP, D)),
        ],
        out_specs=[
            pl.BlockSpec((1, tm, REST_W), lambda b, t: (b, t, 0)),
            pl.BlockSpec((1, tm, GROUP), lambda b, t: (b, t, 0)),
            pl.BlockSpec((1, GROUP, tm), lambda b, t: (b, 0, t)),
            pl.BlockSpec((1, ATTN_HEADS, V_ROWS, tm), lambda b, t: (b, 0, 0, t)),
        ],
        out_shape=[
            jax.ShapeDtypeStruct((B, S, REST_W), F32),
            jax.ShapeDtypeStruct((B, S, GROUP), BF16),
            jax.ShapeDtypeStruct((B, GROUP, S), BF16),
            jax.ShapeDtypeStruct((B, ATTN_HEADS, V_ROWS, S), BF16),
        ],
        compiler_params=_params(("arbitrary", "arbitrary")),
        name="inproj",
    )(h, g, wrest, wk, wqt, wvt)


def _attn_kernel(lq1_ref, lk1_ref, lq2_ref, lk2_ref, gsub_ref, qt_ref, k_ref, vt_ref,
                 o_ref, wq_ref, m_ref, acc_ref, s_ref, smax_ref, *, blk, lambda_init):
    i = pl.program_id(1)

    @pl.when(i == 0)
    def _():
        wq_ref[...] = jnp.zeros(wq_ref.shape, BF16)

    for c in range(N_COMBO):
        rows = slice(c * ATTN_QK_DIM, (c + 1) * ATTN_QK_DIM)
        wq_ref[c, rows, :] = qt_ref[0, rows, :]
    m_ref[...] = jnp.full(m_ref.shape, NEG_INF, F32)
    acc_ref[...] = jnp.zeros(acc_ref.shape, F32)

    def scores(j, c, slot):
        off = pl.multiple_of(j * blk, blk)
        s = jnp.dot(k_ref[0, pl.ds(off, blk), :], wq_ref[c], preferred_element_type=F32)
        s_ref[slot] = s
        smax_ref[slot] = jnp.max(s, axis=0, keepdims=True)

    def consume(j, c, slot, diagonal):
        off = pl.multiple_of(j * blk, blk)
        s = s_ref[slot]
        if diagonal:
            key_pos = lax.broadcasted_iota(jnp.int32, (blk, blk), 0)
            qry_pos = lax.broadcasted_iota(jnp.int32, (blk, blk), 1)
            s = jnp.where(key_pos <= qry_pos, s, NEG_INF)
            s_max = jnp.max(s, axis=0, keepdims=True)
        else:
            s_max = smax_ref[slot]
        m_old = m_ref[c]
        m_new = jnp.maximum(m_old, s_max)
        p = jnp.exp2(s - m_new).astype(BF16)
        corr = jnp.exp2(m_old - m_new)
        vt = vt_ref[0, c // 2, :, pl.ds(off, blk)]
        pv = jnp.dot(vt, p, preferred_element_type=F32)
        acc_ref[c] = acc_ref[c] * corr + pv
        m_ref[c] = m_new

    for c in range(AHEAD):
        scores(0, c, c)

    def body(j, carry):
        for c in range(N_COMBO):
            nxt = c + AHEAD
            if nxt < N_COMBO:
                scores(j, nxt, nxt)
            else:
                scores(j + 1, nxt - N_COMBO, nxt - N_COMBO)
            consume(j, c, c, False)
        return carry

    lax.fori_loop(0, i, body, 0)
    for c in range(N_COMBO):
        if c + AHEAD < N_COMBO:
            scores(i, c + AHEAD, c + AHEAD)
        consume(i, c, c, True)

    lam = (jnp.exp(jnp.sum(lq1_ref[...] * lk1_ref[...], keepdims=True))
           - jnp.exp(jnp.sum(lq2_ref[...] * lk2_ref[...], keepdims=True))
           + lambda_init)
    for h in range(ATTN_HEADS):
        a0 = acc_ref[2 * h]
        a1 = acc_ref[2 * h + 1]
        o0 = a0[0:ATTN_V_DIM] / a0[ATTN_V_DIM:ATTN_V_DIM + 1]
        o1 = a1[0:ATTN_V_DIM] / a1[ATTN_V_DIM:ATTN_V_DIM + 1]
        o = o0 - lam * o1
        ms = jnp.mean(o * o, axis=0, keepdims=True)
        y = o * lax.rsqrt(ms + NORM_EPS) * gsub_ref[...]
        o_ref[0, h * ATTN_V_DIM:(h + 1) * ATTN_V_DIM, :] = y.astype(BF16)


def _attention(lq1, lk1, lq2, lk2, gsub, qt, k, vt, *, blk, lambda_init):
    B, S, _ = k.shape
    small = lambda shape: pl.BlockSpec(shape, lambda b, t: (0,) * len(shape))
    return pl.pallas_call(
        functools.partial(_attn_kernel, blk=blk, lambda_init=lambda_init),
        grid=(B, S // blk),
        in_specs=[
            small((1, ATTN_QK_DIM)), small((1, ATTN_QK_DIM)),
            small((1, ATTN_QK_DIM)), small((1, ATTN_QK_DIM)),
            small((ATTN_V_DIM, blk)),
            pl.BlockSpec((1, GROUP, blk), lambda b, t: (b, 0, t)),
            pl.BlockSpec((1, S, GROUP), lambda b, t: (b, 0, 0), pipeline_mode=pl.Buffered(1)),
            pl.BlockSpec((1, ATTN_HEADS, V_ROWS, S), lambda b, t: (b, 0, 0, 0),
                         pipeline_mode=pl.Buffered(1)),
        ],
        out_specs=pl.BlockSpec((1, GROUP, blk), lambda b, t: (b, 0, t)),
        out_shape=jax.ShapeDtypeStruct((B, GROUP, S), BF16),
        scratch_shapes=[
            pltpu.VMEM((N_COMBO, GROUP, blk), BF16),
            pltpu.VMEM((N_COMBO, 1, blk), F32),
            pltpu.VMEM((N_COMBO, V_ROWS, blk), F32),
            pltpu.VMEM((N_COMBO, blk, blk), F32),
            pltpu.VMEM((N_COMBO, 1, blk), F32),
        ],
        compiler_params=_params(("arbitrary", "arbitrary")),
        name="attn",
    )(lq1, lk1, lq2, lk2, gsub, qt, k, vt)


def _linear_scan(a, u):
    rows = a.shape[0]
    row = lax.broadcasted_iota(jnp.int32, (rows, 1), 0)
    d = 1
    while d < rows:
        keep = row >= d
        a_prev = pltpu.roll(a, d, 0)
        u_prev = pltpu.roll(u, d, 0)
        u = jnp.where(keep, a * u_prev + u, u)
        a = jnp.where(keep, a * a_prev, a)
        d *= 2
    return a, u


def _causal_taps(ext_ref, cur, w_ref, halo, taps, tm):
    ext_ref[halo:halo + tm, :] = cur
    acc = None
    for k in range(taps):
        term = w_ref[k:k + 1, :] * ext_ref[pl.ds(halo - (taps - 1) + k, tm), :]
        acc = term if acc is None else acc + term
    ext_ref[0:halo, :] = cur[tm - halo:tm]
    return acc


def _mixer_kernel(h_ref, rest_ref, ybt_ref, lcw_ref, lcb_ref, wgate_ref, bgate_ref, lam_ref,
                  cw_ref, cb_ref, lng_ref, lnb_ref, scw_ref, wout_ref, gpost_ref,
                  out_ref, axext_ref, cext_ref, dext_ref, hstate_ref, *, tm):
    G = GROUP

    @pl.when(pl.program_id(1) == 0)
    def _():
        axext_ref[...] = jnp.zeros(axext_ref.shape, F32)
        cext_ref[...] = jnp.zeros(cext_ref.shape, F32)
        dext_ref[...] = jnp.zeros(dext_ref.shape, F32)
        hstate_ref[...] = jnp.zeros(hstate_ref.shape, F32)

    ax = rest_ref[0, :, 0:G]
    xa = _causal_taps(axext_ref, ax, lcw_ref, 8, LRU_CONV, tm) + lcb_ref[...]
    gates = jnp.dot(xa.astype(BF16), wgate_ref[...], preferred_element_type=F32) + bgate_ref[...]
    r = jax.nn.sigmoid(gates[:, 0:G])
    ig = jax.nn.sigmoid(gates[:, G:2 * G])
    neg_lam = -lam_ref[...]
    softplus = jnp.maximum(neg_lam, 0.0) + jnp.log1p(jnp.exp(-jnp.abs(neg_lam)))
    log_a = (-LRU_C) * r * softplus
    a = jnp.exp(log_a)
    u = jnp.sqrt(1.0 - a * a) * (ig * xa)
    a_cum, h_loc = _linear_scan(a, u)
    hh = h_loc + a_cum * hstate_ref[0:1, :]
    hstate_ref[0:1, :] = hh[tm - 1:tm]
    ya = hh * jax.nn.gelu(rest_ref[0, :, G:2 * G], approximate=True)

    c = rest_ref[0, :, 2 * G:3 * G] * jax.nn.sigmoid(rest_ref[0, :, 3 * G:4 * G])
    cc = _causal_taps(cext_ref, c, cw_ref, 32, CONF_KERNEL, tm) + cb_ref[...]
    mu = jnp.mean(cc, axis=-1, keepdims=True)
    xc = cc - mu
    var = jnp.mean(xc * xc, axis=-1, keepdims=True)
    ln = xc * lax.rsqrt(var + LN_EPS) * lng_ref[...] + lnb_ref[...]
    yc = ln * jax.nn.sigmoid(ln)

    dd = rest_ref[0, :, 6 * G:7 * G] * rest_ref[0, :, 4 * G:5 * G]
    yd = rest_ref[0, :, 5 * G:6 * G] * _causal_taps(dext_ref, dd, scw_ref, 8, SC_KERNEL, tm)

    m = jnp.dot(ya.astype(BF16), wout_ref[0:G, :], preferred_element_type=F32)
    m += lax.dot_general(ybt_ref[0], wout_ref[G:2 * G, :], TN_DIMS, preferred_element_type=F32)
    m += jnp.dot(yc.astype(BF16), wout_ref[2 * G:3 * G, :], preferred_element_type=F32)
    m += jnp.dot(yd.astype(BF16), wout_ref[3 * G:4 * G, :], preferred_element_type=F32)
    out_ref[0] = h_ref[0] + _rms(m, gpost_ref[...])


def _mixer(h, rest, ybt, lcw, lcb, wgate, bgate, lam, cw, cb, lng, lnb, scw, wout, gpost, *, tm):
    B, S, D = h.shape
    G = GROUP
    return pl.pallas_call(
        functools.partial(_mixer_kernel, tm=tm),
        grid=(B, S // tm),
        in_specs=[
            pl.BlockSpec((1, tm, D), lambda b, t: (b, t, 0)),
            pl.BlockSpec((1, tm, REST_W), lambda b, t: (b, t, 0)),
            pl.BlockSpec((1, G, tm), lambda b, t: (b, 0, t)),
            _const_spec((LRU_CONV, G)), _const_spec((1, G)),
            _const_spec((G, 2 * G)), _const_spec((1, 2 * G)), _const_spec((1, G)),
            _const_spec((CONF_KERNEL, G)), _const_spec((1, G)),
            _const_spec((1, G)), _const_spec((1, G)),
            _const_spec((SC_KERNEL, G)),
            _const_spec((D, D)), _const_spec((1, D)),
        ],
        out_specs=pl.BlockSpec((1, tm, D), lambda b, t: (b, t, 0)),
        out_shape=jax.ShapeDtypeStruct((B, S, D), F32),
        scratch_shapes=[
            pltpu.VMEM((8 + tm, G), F32),
            pltpu.VMEM((32 + tm, G), F32),
            pltpu.VMEM((8 + tm, G), F32),
            pltpu.VMEM((8, G), F32),
        ],
        compiler_params=_params(("arbitrary", "arbitrary")),
        name="mixer",
    )(h, rest, ybt, lcw, lcb, wgate, bgate, lam, cw, cb, lng, lnb, scw, wout, gpost)


def _ffn_kernel(h_ref, gpre_ref, wup_ref, cw_ref, cb_ref, wdown_ref, gpost_ref,
                out_ref, gext_ref, *, tm, chunk):
    @pl.when(pl.program_id(1) == 0)
    def _():
        gext_ref[...] = jnp.zeros(gext_ref.shape, F32)

    x = h_ref[0]
    yb = _rms(x, gpre_ref[...]).astype(BF16)
    f = jnp.zeros((tm, D_MODEL), F32)
    for c in range(D_FF // chunk):
        lo = c * chunk
        u = jnp.dot(yb, wup_ref[:, lo:lo + chunk], preferred_element_type=F32)
        g = jnp.dot(yb, wup_ref[:, D_FF + lo:D_FF + lo + chunk], preferred_element_type=F32)
        ext = gext_ref.at[c]
        gc = _causal_taps(ext, g, cw_ref.at[:, lo:lo + chunk], 8, FFN_KERNEL, tm)
        gc = gc + cb_ref[:, lo:lo + chunk]
        act = (jax.nn.gelu(gc, approximate=True) * u).astype(BF16)
        f += jnp.dot(act, wdown_ref[lo:lo + chunk, :], preferred_element_type=F32)
    out_ref[0] = x + _rms(f, gpost_ref[...])


def _ffn(h, gpre, wup, cw, cb, wdown, gpost, *, tm, chunk):
    B, S, D = h.shape
    return pl.pallas_call(
        functools.partial(_ffn_kernel, tm=tm, chunk=chunk),
        grid=(B, S // tm),
        in_specs=[
            pl.BlockSpec((1, tm, D), lambda b, t: (b, t, 0)),
            _const_spec((1, D)),
            _const_spec((D, 2 * D_FF)),
            _const_spec((FFN_KERNEL, D_FF)), _const_spec((1, D_FF)),
            _const_spec((D_FF, D)), _const_spec((1, D)),
        ],
        out_specs=pl.BlockSpec((1, tm, D), lambda b, t: (b, t, 0)),
        out_shape=jax.ShapeDtypeStruct((B, S, D), F32),
        scratch_shapes=[pltpu.VMEM((D_FF // chunk, 8 + tm, chunk), F32)],
        compiler_params=_params(("arbitrary", "arbitrary")),
        name="ffn",
    )(h, gpre, wup, cw, cb, wdown, gpost)


def _block_diag(w):
    heads, hd, _ = w.shape
    eye = jnp.eye(heads, dtype=w.dtype)
    return (eye[:, None, :, None] * w[:, :, None, :]).reshape(heads * hd, heads * hd)


def _pick_tile(S, want):
    t = min(want, S)
    while S % t:
        t //= 2
    return t


def kernel(x, norm_mix_pre, norm_mix_post, norm_ffn_pre, norm_ffn_post, w_in, lru_conv_w, lru_conv_b, lru_wa, lru_ba, lru_wx, lru_bx, lru_lambda, attn_lq1, attn_lk1, attn_lq2, attn_lk2, attn_subln, conf_dw_w, conf_dw_b, conf_ln_g, conf_ln_b, sc_conv_w, w_out, ffn_w_up, ffn_conv_w, ffn_conv_b, ffn_w_down):
    B, S, D = x.shape
    assert D == D_MODEL
    depth = w_in.shape[0]
    G = GROUP
    tm = _pick_tile(S, 512)
    blk = _pick_tile(S, 256)
    assert tm >= 32 and blk % 128 == 0
    row = lambda v: v.reshape(1, -1)

    h = x
    for l in range(depth):
        lambda_init = 0.8 - 0.6 * math.exp(-0.3 * l)
        wi = w_in[l]
        wrest = jnp.concatenate([wi[:, 0:2 * G], wi[:, 5 * G:10 * G]], axis=1).astype(BF16)
        wk = wi[:, 3 * G:4 * G].astype(BF16)
        wqt = wi[:, 2 * G:3 * G].T.astype(BF16)
        wvt = wi[:, 4 * G:5 * G].T.astype(BF16)
        rest, k, qt, vt = _inproj(h, row(norm_mix_pre[l]), wrest, wk, wqt, wvt, tm=tm)

        gsub = jnp.broadcast_to((attn_subln[l] * (1.0 - lambda_init))[:, None], (ATTN_V_DIM, blk))
        ybt = _attention(row(attn_lq1[l]), row(attn_lk1[l]), row(attn_lq2[l]), row(attn_lk2[l]),
                         gsub, qt, k, vt, blk=blk, lambda_init=lambda_init)

        wgate = jnp.concatenate([_block_diag(lru_wa[l]), _block_diag(lru_wx[l])], axis=1).astype(BF16)
        bgate = jnp.concatenate([lru_ba[l], lru_bx[l]]).reshape(1, 2 * G)
        h = _mixer(h, rest, ybt, lru_conv_w[l], row(lru_conv_b[l]), wgate, bgate, row(lru_lambda[l]),
                   conf_dw_w[l], row(conf_dw_b[l]), row(conf_ln_g[l]), row(conf_ln_b[l]),
                   sc_conv_w[l], w_out[l].astype(BF16), row(norm_mix_post[l]), tm=tm)

        h = _ffn(h, row(norm_ffn_pre[l]), ffn_w_up[l].astype(BF16), ffn_conv_w[l],
                 row(ffn_conv_b[l]), ffn_w_down[l].astype(BF16), row(norm_ffn_post[l]),
                 tm=tm, chunk=256)
    return h
```

```python
import functools
import math

import jax
import jax.numpy as jnp
from jax import lax
from jax.experimental import pallas as pl
from jax.experimental.pallas import tpu as pltpu

F32 = jnp.float32
BF16 = jnp.bfloat16

D_MODEL = 1024
GROUP = D_MODEL // 4
LRU_HEADS = 4
LRU_HEAD_DIM = GROUP // LRU_HEADS
LRU_CONV = 4
LRU_C = 8.0
ATTN_HEADS = 4
ATTN_V_DIM = GROUP // ATTN_HEADS
ATTN_QK_DIM = ATTN_V_DIM // 2
CONF_KERNEL = 31
SC_KERNEL = 3
D_FF = 11 * D_MODEL // 4
FFN_KERNEL = 3
NORM_EPS = 1e-6
LN_EPS = 1e-5
NEG_INF = -1e30

N_COMBO = 2 * ATTN_HEADS
AHEAD = 5
AHEAD_BIG = 2
V_ROWS = ATTN_V_DIM + 16
REST_W = 7 * GROUP
LOG2E = 1.4426950408889634
VMEM_LIMIT_BYTES = 56 * 1024 * 1024

NT_DIMS = (((1,), (1,)), ((), ()))
TN_DIMS = (((0,), (0,)), ((), ()))


def _rms(x, g):
    return x * lax.rsqrt(jnp.mean(x * x, axis=-1, keepdims=True) + NORM_EPS) * g


def _gelu_tanh_times(x, y):
    c0 = math.sqrt(2.0 / math.pi)
    inner = x * (c0 + (c0 * 0.044715) * (x * x))
    return (x * y) * (0.5 * jnp.tanh(inner) + 0.5)


def _const_spec(shape):
    nd = len(shape)
    return pl.BlockSpec(shape, lambda b, t: (0,) * nd, pipeline_mode=pl.Buffered(1))


def _params(sem):
    return pltpu.CompilerParams(dimension_semantics=sem, vmem_limit_bytes=VMEM_LIMIT_BYTES)


def _inproj_kernel(h_ref, g_ref, wrest_ref, wk_ref, wqt_ref, wvt_ref,
                   rest_ref, k_ref, qt_ref, vt_ref, *, tm):
    yb = _rms(h_ref[0], g_ref[...]).astype(BF16)
    rest_ref[0] = jnp.dot(yb, wrest_ref[...], preferred_element_type=F32)
    k_ref[0] = jnp.dot(yb, wk_ref[...], preferred_element_type=F32).astype(BF16)
    qt = lax.dot_general(wqt_ref[...], yb, NT_DIMS, preferred_element_type=F32)
    qt_ref[0] = (qt * (ATTN_QK_DIM ** -0.5 * LOG2E)).astype(BF16)
    vt = lax.dot_general(wvt_ref[...], yb, NT_DIMS, preferred_element_type=F32).astype(BF16)
    ones = jnp.ones((V_ROWS - ATTN_V_DIM, tm), BF16)
    for h in range(ATTN_HEADS):
        vt_ref[0, h, 0:ATTN_V_DIM, :] = vt[h * ATTN_V_DIM:(h + 1) * ATTN_V_DIM]
        vt_ref[0, h, ATTN_V_DIM:V_ROWS, :] = ones


def _inproj(h, g, wrest, wk, wqt, wvt, *, tm):
    B, S, D = h.shape
    return pl.pallas_call(
        functools.partial(_inproj_kernel, tm=tm),
        grid=(B, S // tm),
        in_specs=[
            pl.BlockSpec((1, tm, D), lambda b, t: (b, t, 0)),
            _const_spec((1, D)),
            _const_spec((D, REST_W)),
            _const_spec((D, GROUP)),
            _const_spec((GROUP, D)),
            _const_spec((GROUP, D)),
        ],
        out_specs=[
            pl.BlockSpec((1, tm, REST_W), lambda b, t: (b, t, 0)),
            pl.BlockSpec((1, tm, GROUP), lambda b, t: (b, t, 0)),
            pl.BlockSpec((1, GROUP, tm), lambda b, t: (b, 0, t)),
            pl.BlockSpec((1, ATTN_HEADS, V_ROWS, tm), lambda b, t: (b, 0, 0, t)),
        ],
        out_shape=[
            jax.ShapeDtypeStruct((B, S, REST_W), F32),
            jax.ShapeDtypeStruct((B, S, GROUP), BF16),
            jax.ShapeDtypeStruct((B, GROUP, S), BF16),
            jax.ShapeDtypeStruct((B, ATTN_HEADS, V_ROWS, S), BF16),
        ],
        compiler_params=_params(("arbitrary", "arbitrary")),
        name="inproj",
    )(h, g, wrest, wk, wqt, wvt)


def _attn_kernel(lq1_ref, lk1_ref, lq2_ref, lk2_ref, gsub_ref, qt_ref, k_ref, vt_ref,
                 o_ref, wq_ref, m_ref, acc_ref, sbig_ref, ssml_ref, smax_ref,
                 *, blk, big, lambda_init):
    i = pl.program_id(1)

    @pl.when(i == 0)
    def _():
        wq_ref[...] = jnp.zeros(wq_ref.shape, BF16)

    for c in range(N_COMBO):
        rows = slice(c * ATTN_QK_DIM, (c + 1) * ATTN_QK_DIM)
        wq_ref[c, rows, :] = qt_ref[0, rows, :]
    m_ref[...] = jnp.full(m_ref.shape, NEG_INF, F32)
    acc_ref[...] = jnp.zeros(acc_ref.shape, F32)

    def scores(s_ref, key_off, rows, c, visible_below):
        s = jnp.dot(k_ref[0, pl.ds(key_off, rows), :], wq_ref[c], preferred_element_type=F32)
        if visible_below is not None:
            delta = (lax.broadcasted_iota(jnp.int32, (rows, blk), 0)
                     - lax.broadcasted_iota(jnp.int32, (rows, blk), 1))
            s = jnp.where(delta <= visible_below, s, NEG_INF)
        s_ref[c] = s
        smax_ref[c] = jnp.max(s, axis=0, keepdims=True)

    def consume(s_ref, key_off, rows, c):
        m_old = m_ref[c]
        m_new = jnp.maximum(m_old, smax_ref[c])
        p = jnp.exp2(s_ref[c] - m_new).astype(BF16)
        corr = jnp.exp2(m_old - m_new)
        vt = vt_ref[0, c // 2, :, pl.ds(key_off, rows)]
        pv = jnp.dot(vt, p, preferred_element_type=F32)
        acc_ref[c] = acc_ref[c] * corr + pv
        m_ref[c] = m_new

    def phase(s_ref, rows, first_key, n_steps, ahead, masked):
        def key_off(t):
            return pl.multiple_of(first_key + t * rows, rows)

        def bound(t):
            return (i * blk - key_off(t)) if masked else None

        def step(t, prefetch_next_step):
            for c in range(N_COMBO):
                nxt = c + ahead
                if nxt < N_COMBO:
                    scores(s_ref, key_off(t), rows, nxt, bound(t))
                elif prefetch_next_step:
                    scores(s_ref, key_off(t + 1), rows, nxt - N_COMBO, bound(t + 1))
                consume(s_ref, key_off(t), rows, c)

        for c in range(ahead):
            scores(s_ref, key_off(0), rows, c, bound(0))

        def body(t, carry):
            step(t, True)
            return carry

        lax.fori_loop(0, n_steps - 1, body, 0)
        step(n_steps - 1, False)

    per_big = big // blk
    n_big = i // per_big

    @pl.when(n_big > 0)
    def _():
        phase(sbig_ref, big, 0, n_big, AHEAD_BIG, False)

    phase(ssml_ref, blk, n_big * big, i - n_big * per_big + 1, AHEAD, True)

    lam = (jnp.exp(jnp.sum(lq1_ref[...] * lk1_ref[...], keepdims=True))
           - jnp.exp(jnp.sum(lq2_ref[...] * lk2_ref[...], keepdims=True))
           + lambda_init)
    for h in range(ATTN_HEADS):
        a0 = acc_ref[2 * h]
        a1 = acc_ref[2 * h + 1]
        o0 = a0[0:ATTN_V_DIM] / a0[ATTN_V_DIM:ATTN_V_DIM + 1]
        o1 = a1[0:ATTN_V_DIM] / a1[ATTN_V_DIM:ATTN_V_DIM + 1]
        o = o0 - lam * o1
        ms = jnp.mean(o * o, axis=0, keepdims=True)
        y = o * lax.rsqrt(ms + NORM_EPS) * gsub_ref[...]
        o_ref[0, h * ATTN_V_DIM:(h + 1) * ATTN_V_DIM, :] = y.astype(BF16)


def _attention(lq1, lk1, lq2, lk2, gsub, qt, k, vt, *, blk, big, lambda_init):
    B, S, _ = k.shape
    small = lambda shape: pl.BlockSpec(shape, lambda b, t: (0,) * len(shape))
    return pl.pallas_call(
        functools.partial(_attn_kernel, blk=blk, big=big, lambda_init=lambda_init),
        grid=(B, S // blk),
        in_specs=[
            small((1, ATTN_QK_DIM)), small((1, ATTN_QK_DIM)),
            small((1, ATTN_QK_DIM)), small((1, ATTN_QK_DIM)),
            small((ATTN_V_DIM, blk)),
            pl.BlockSpec((1, GROUP, blk), lambda b, t: (b, 0, t)),
            pl.BlockSpec((1, S, GROUP), lambda b, t: (b, 0, 0), pipeline_mode=pl.Buffered(1)),
            pl.BlockSpec((1, ATTN_HEADS, V_ROWS, S), lambda b, t: (b, 0, 0, 0),
                         pipeline_mode=pl.Buffered(1)),
        ],
        out_specs=pl.BlockSpec((1, GROUP, blk), lambda b, t: (b, 0, t)),
        out_shape=jax.ShapeDtypeStruct((B, GROUP, S), BF16),
        scratch_shapes=[
            pltpu.VMEM((N_COMBO, GROUP, blk), BF16),
            pltpu.VMEM((N_COMBO, 1, blk), F32),
            pltpu.VMEM((N_COMBO, V_ROWS, blk), F32),
            pltpu.VMEM((N_COMBO, big, blk), F32),
            pltpu.VMEM((N_COMBO, blk, blk), F32),
            pltpu.VMEM((N_COMBO, 1, blk), F32),
        ],
        compiler_params=_params(("arbitrary", "arbitrary")),
        name="attn",
    )(lq1, lk1, lq2, lk2, gsub, qt, k, vt)


def _linear_scan(a, u):
    rows = a.shape[0]
    row = lax.broadcasted_iota(jnp.int32, (rows, 1), 0)
    d = 1
    while d < rows:
        keep = row >= d
        a_prev = pltpu.roll(a, d, 0)
        u_prev = pltpu.roll(u, d, 0)
        u = jnp.where(keep, a * u_prev + u, u)
        a = jnp.where(keep, a * a_prev, a)
        d *= 2
    return a, u


def _causal_taps(cur, prev_ref, w_ref, taps):
    halo = prev_ref.shape[0]
    tm = cur.shape[0]
    assert halo % 8 == 0 and halo >= taps - 1
    ext = jnp.concatenate([prev_ref[...], cur], axis=0)
    prev_ref[...] = cur[tm - halo:tm]
    acc = None
    for r in range(min(8, taps)):
        rot = ext if r == 0 else pltpu.roll(ext, r, 0)
        for a in range((taps - 1 - r) // 8 + 1):
            k = taps - 1 - (8 * a + r)
            term = w_ref[k:k + 1, :] * rot[halo - 8 * a:halo - 8 * a + tm]
            acc = term if acc is None else acc + term
    return acc


def _mixer_kernel(h_ref, rest_ref, ybt_ref, lcw_ref, lcb_ref, wgate_ref, bgate_ref, lam_ref,
                  cw_ref, cb_ref, lng_ref, lnb_ref, scw_ref, wout_ref, gpost_ref,
                  out_ref, axprev_ref, cprev_ref, dprev_ref, hstate_ref, *, tm):
    G = GROUP

    @pl.when(pl.program_id(1) == 0)
    def _():
        axprev_ref[...] = jnp.zeros(axprev_ref.shape, F32)
        cprev_ref[...] = jnp.zeros(cprev_ref.shape, F32)
        dprev_ref[...] = jnp.zeros(dprev_ref.shape, F32)
        hstate_ref[...] = jnp.zeros(hstate_ref.shape, F32)

    xa = _causal_taps(rest_ref[0, :, 0:G], axprev_ref, lcw_ref, LRU_CONV) + lcb_ref[...]
    gates = jnp.dot(xa.astype(BF16), wgate_ref[...], preferred_element_type=F32) + bgate_ref[...]
    r = jax.nn.sigmoid(gates[:, 0:G])
    ig = jax.nn.sigmoid(gates[:, G:2 * G])
    neg_lam = -lam_ref[...]
    softplus = jnp.maximum(neg_lam, 0.0) + jnp.log1p(jnp.exp(-jnp.abs(neg_lam)))
    log_a = (-LRU_C) * r * softplus
    a = jnp.exp(log_a)
    u = jnp.sqrt(1.0 - a * a) * (ig * xa)
    a_cum, h_loc = _linear_scan(a, u)
    hh = h_loc + a_cum * hstate_ref[0:1, :]
    hstate_ref[0:1, :] = hh[tm - 1:tm]
    ya = _gelu_tanh_times(rest_ref[0, :, G:2 * G], hh)

    c = rest_ref[0, :, 2 * G:3 * G] * jax.nn.sigmoid(rest_ref[0, :, 3 * G:4 * G])
    cc = _causal_taps(c, cprev_ref, cw_ref, CONF_KERNEL) + cb_ref[...]
    mu = jnp.mean(cc, axis=-1, keepdims=True)
    xc = cc - mu
    var = jnp.mean(xc * xc, axis=-1, keepdims=True)
    ln = xc * lax.rsqrt(var + LN_EPS) * lng_ref[...] + lnb_ref[...]
    yc = ln * jax.nn.sigmoid(ln)

    dd = rest_ref[0, :, 6 * G:7 * G] * rest_ref[0, :, 4 * G:5 * G]
    yd = rest_ref[0, :, 5 * G:6 * G] * _causal_taps(dd, dprev_ref, scw_ref, SC_KERNEL)

    m = jnp.dot(ya.astype(BF16), wout_ref[0:G, :], preferred_element_type=F32)
    m += lax.dot_general(ybt_ref[0], wout_ref[G:2 * G, :], TN_DIMS, preferred_element_type=F32)
    m += jnp.dot(yc.astype(BF16), wout_ref[2 * G:3 * G, :], preferred_element_type=F32)
    m += jnp.dot(yd.astype(BF16), wout_ref[3 * G:4 * G, :], preferred_element_type=F32)
    out_ref[0] = h_ref[0] + _rms(m, gpost_ref[...])


def _mixer(h, rest, ybt, lcw, lcb, wgate, bgate, lam, cw, cb, lng, lnb, scw, wout, gpost, *, tm):
    B, S, D = h.shape
    G = GROUP
    return pl.pallas_call(
        functools.partial(_mixer_kernel, tm=tm),
        grid=(B, S // tm),
        in_specs=[
            pl.BlockSpec((1, tm, D), lambda b, t: (b, t, 0)),
            pl.BlockSpec((1, tm, REST_W), lambda b, t: (b, t, 0)),
            pl.BlockSpec((1, G, tm), lambda b, t: (b, 0, t)),
            _const_spec((LRU_CONV, G)), _const_spec((1, G)),
            _const_spec((G, 2 * G)), _const_spec((1, 2 * G)), _const_spec((1, G)),
            _const_spec((CONF_KERNEL, G)), _const_spec((1, G)),
            _const_spec((1, G)), _const_spec((1, G)),
            _const_spec((SC_KERNEL, G)),
            _const_spec((D, D)), _const_spec((1, D)),
        ],
        out_specs=pl.BlockSpec((1, tm, D), lambda b, t: (b, t, 0)),
        out_shape=jax.ShapeDtypeStruct((B, S, D), F32),
        scratch_shapes=[
            pltpu.VMEM((8, G), F32),
            pltpu.VMEM((32, G), F32),
            pltpu.VMEM((8, G), F32),
            pltpu.VMEM((8, G), F32),
        ],
        compiler_params=_params(("arbitrary", "arbitrary")),
        name="mixer",
    )(h, rest, ybt, lcw, lcb, wgate, bgate, lam, cw, cb, lng, lnb, scw, wout, gpost)


def _ffn_kernel(h_ref, gpre_ref, wup_ref, cw_ref, cb_ref, wdown_ref, gpost_ref,
                out_ref, gprev_ref, *, tm, chunk):
    @pl.when(pl.program_id(1) == 0)
    def _():
        gprev_ref[...] = jnp.zeros(gprev_ref.shape, F32)

    x = h_ref[0]
    yb = _rms(x, gpre_ref[...]).astype(BF16)
    n_chunks = D_FF // chunk

    def up(c):
        lo = c * chunk
        u = jnp.dot(yb, wup_ref[:, lo:lo + chunk], preferred_element_type=F32)
        g = jnp.dot(yb, wup_ref[:, D_FF + lo:D_FF + lo + chunk], preferred_element_type=F32)
        return u, g

    f = jnp.zeros((tm, D_MODEL), F32)
    nxt = up(0)
    for c in range(n_chunks):
        lo = c * chunk
        u, g = nxt
        if c + 1 < n_chunks:
            nxt = up(c + 1)
        gc = _causal_taps(g, gprev_ref.at[c], cw_ref.at[:, lo:lo + chunk], FFN_KERNEL)
        gc = gc + cb_ref[:, lo:lo + chunk]
        act = (_gelu_tanh_times(gc, u)).astype(BF16)
        f += jnp.dot(act, wdown_ref[lo:lo + chunk, :], preferred_element_type=F32)
    out_ref[0] = x + _rms(f, gpost_ref[...])


def _ffn(h, gpre, wup, cw, cb, wdown, gpost, *, tm, chunk):
    B, S, D = h.shape
    return pl.pallas_call(
        functools.partial(_ffn_kernel, tm=tm, chunk=chunk),
        grid=(B, S // tm),
        in_specs=[
            pl.BlockSpec((1, tm, D), lambda b, t: (b, t, 0)),
            _const_spec((1, D)),
            _const_spec((D, 2 * D_FF)),
            _const_spec((FFN_KERNEL, D_FF)), _const_spec((1, D_FF)),
            _const_spec((D_FF, D)), _const_spec((1, D)),
        ],
        out_specs=pl.BlockSpec((1, tm, D), lambda b, t: (b, t, 0)),
        out_shape=jax.ShapeDtypeStruct((B, S, D), F32),
        scratch_shapes=[pltpu.VMEM((D_FF // chunk, 8, chunk), F32)],
        compiler_params=_params(("arbitrary", "arbitrary")),
        name="ffn",
    )(h, gpre, wup, cw, cb, wdown, gpost)


def _block_diag(w):
    heads, hd, _ = w.shape
    eye = jnp.eye(heads, dtype=w.dtype)
    return (eye[:, None, :, None] * w[:, :, None, :]).reshape(heads * hd, heads * hd)


def _pick_tile(S, want):
    t = min(want, S)
    while S % t:
        t //= 2
    return t


def kernel(x, norm_mix_pre, norm_mix_post, norm_ffn_pre, norm_ffn_post, w_in, lru_conv_w, lru_conv_b, lru_wa, lru_ba, lru_wx, lru_bx, lru_lambda, attn_lq1, attn_lk1, attn_lq2, attn_lk2, attn_subln, conf_dw_w, conf_dw_b, conf_ln_g, conf_ln_b, sc_conv_w, w_out, ffn_w_up, ffn_conv_w, ffn_conv_b, ffn_w_down):
    B, S, D = x.shape
    assert D == D_MODEL
    depth = w_in.shape[0]
    G = GROUP
    tm = _pick_tile(S, 512)
    blk = _pick_tile(S, 256)
    big = _pick_tile(S, 1024)
    assert tm >= 32 and blk % 128 == 0 and big % blk == 0
    row = lambda v: v.reshape(1, -1)

    h = x
    for l in range(depth):
        lambda_init = 0.8 - 0.6 * math.exp(-0.3 * l)
        wi = w_in[l]
        wrest = jnp.concatenate([wi[:, 0:2 * G], wi[:, 5 * G:10 * G]], axis=1).astype(BF16)
        wk = wi[:, 3 * G:4 * G].astype(BF16)
        wqt = wi[:, 2 * G:3 * G].T.astype(BF16)
        wvt = wi[:, 4 * G:5 * G].T.astype(BF16)
        rest, k, qt, vt = _inproj(h, row(norm_mix_pre[l]), wrest, wk, wqt, wvt, tm=tm)

        gsub = jnp.broadcast_to((attn_subln[l] * (1.0 - lambda_init))[:, None], (ATTN_V_DIM, blk))
        ybt = _attention(row(attn_lq1[l]), row(attn_lk1[l]), row(attn_lq2[l]), row(attn_lk2[l]),
                         gsub, qt, k, vt, blk=blk, big=big, lambda_init=lambda_init)

        wgate = jnp.concatenate([_block_diag(lru_wa[l]), _block_diag(lru_wx[l])], axis=1).astype(BF16)
        bgate = jnp.concatenate([lru_ba[l], lru_bx[l]]).reshape(1, 2 * G)
        h = _mixer(h, rest, ybt, lru_conv_w[l], row(lru_conv_b[l]), wgate, bgate, row(lru_lambda[l]),
                   conf_dw_w[l], row(conf_dw_b[l]), row(conf_ln_g[l]), row(conf_ln_b[l]),
                   sc_conv_w[l], w_out[l].astype(BF16), row(norm_mix_post[l]), tm=tm)

        h = _ffn(h, row(norm_ffn_pre[l]), ffn_w_up[l].astype(BF16), ffn_conv_w[l],
                 row(ffn_conv_b[l]), ffn_w_down[l].astype(BF16), row(norm_ffn_post[l]),
                 tm=tm, chunk=256)
    return h
```

```python
import functools
import math

import jax
import jax.numpy as jnp
from jax import lax
from jax.experimental import pallas as pl
from jax.experimental.pallas import tpu as pltpu

F32 = jnp.float32
BF16 = jnp.bfloat16

D_MODEL = 1024
GROUP = D_MODEL // 4
LRU_HEADS = 4
LRU_HEAD_DIM = GROUP // LRU_HEADS
LRU_CONV = 4
LRU_C = 8.0
ATTN_HEADS = 4
ATTN_V_DIM = GROUP // ATTN_HEADS
ATTN_QK_DIM = ATTN_V_DIM // 2
CONF_KERNEL = 31
SC_KERNEL = 3
D_FF = 11 * D_MODEL // 4
FFN_KERNEL = 3
NORM_EPS = 1e-6
LN_EPS = 1e-5
NEG_INF = -1e30

N_COMBO = 2 * ATTN_HEADS
AHEAD = 5
AHEAD_BIG = 2
FFN_AHEAD = 1
V_ROWS = ATTN_V_DIM + 16
REST_W = 7 * GROUP
LOG2E = 1.4426950408889634
VMEM_LIMIT_BYTES = 56 * 1024 * 1024

TN_DIMS = (((0,), (0,)), ((), ()))


def _rms(x, g):
    return x * lax.rsqrt(jnp.mean(x * x, axis=-1, keepdims=True) + NORM_EPS) * g


def _gelu_tanh_times(x, y):
    c0 = math.sqrt(2.0 / math.pi)
    inner = x * (c0 + (c0 * 0.044715) * (x * x))
    return (x * y) * (0.5 * jnp.tanh(inner) + 0.5)


def _layer_spec(l, shape):
    nd = len(shape)
    return pl.BlockSpec((None,) + tuple(shape), lambda b, t: (l,) + (0,) * nd,
                        pipeline_mode=pl.Buffered(1))


def _params(sem):
    return pltpu.CompilerParams(dimension_semantics=sem, vmem_limit_bytes=VMEM_LIMIT_BYTES)


def _inproj_kernel(h_ref, g_ref, w_ref, rest_ref, k_ref, qt_ref, vt_ref, *, tm):
    G = GROUP
    yb = _rms(h_ref[0], g_ref[...]).astype(BF16)

    def proj(lo, hi):
        return jnp.dot(yb, w_ref[:, lo:hi], preferred_element_type=F32)

    rest_ref[0, :, 0:2 * G] = proj(0, 2 * G)
    rest_ref[0, :, 2 * G:REST_W] = proj(5 * G, 10 * G)
    k_ref[0] = proj(3 * G, 4 * G).astype(BF16)
    q = proj(2 * G, 3 * G) * (ATTN_QK_DIM ** -0.5 * LOG2E)
    qt_ref[0] = q.astype(BF16).T
    vt = proj(4 * G, 5 * G).astype(BF16).T
    ones = jnp.ones((V_ROWS - ATTN_V_DIM, tm), BF16)
    for h in range(ATTN_HEADS):
        vt_ref[0, h, 0:ATTN_V_DIM, :] = vt[h * ATTN_V_DIM:(h + 1) * ATTN_V_DIM]
        vt_ref[0, h, ATTN_V_DIM:V_ROWS, :] = ones


def _inproj(l, h, g, w_in, *, tm):
    B, S, D = h.shape
    return pl.pallas_call(
        functools.partial(_inproj_kernel, tm=tm),
        grid=(B, S // tm),
        in_specs=[
            pl.BlockSpec((1, tm, D), lambda b, t: (b, t, 0)),
            _layer_spec(l, (1, D)),
            _layer_spec(l, (D, 10 * GROUP)),
        ],
        out_specs=[
            pl.BlockSpec((1, tm, REST_W), lambda b, t: (b, t, 0)),
            pl.BlockSpec((1, tm, GROUP), lambda b, t: (b, t, 0)),
            pl.BlockSpec((1, GROUP, tm), lambda b, t: (b, 0, t)),
            pl.BlockSpec((1, ATTN_HEADS, V_ROWS, tm), lambda b, t: (b, 0, 0, t)),
        ],
        out_shape=[
            jax.ShapeDtypeStruct((B, S, REST_W), F32),
            jax.ShapeDtypeStruct((B, S, GROUP), BF16),
            jax.ShapeDtypeStruct((B, GROUP, S), BF16),
            jax.ShapeDtypeStruct((B, ATTN_HEADS, V_ROWS, S), BF16),
        ],
        compiler_params=_params(("arbitrary", "arbitrary")),
        name="inproj",
    )(h, g, w_in)


def _attn_kernel(lq1_ref, lk1_ref, lq2_ref, lk2_ref, gsub_ref, qt_ref, k_ref, vt_ref,
                 o_ref, wq_ref, m_ref, acc_ref, sbig_ref, ssml_ref, smax_ref,
                 *, blk, big, lambda_init):
    i = pl.program_id(1)

    @pl.when(i == 0)
    def _():
        wq_ref[...] = jnp.zeros(wq_ref.shape, BF16)

    for c in range(N_COMBO):
        rows = slice(c * ATTN_QK_DIM, (c + 1) * ATTN_QK_DIM)
        wq_ref[c, rows, :] = qt_ref[0, rows, :]
    m_ref[...] = jnp.full(m_ref.shape, NEG_INF, F32)
    acc_ref[...] = jnp.zeros(acc_ref.shape, F32)

    def scores(s_ref, key_off, rows, c, visible_below):
        s = jnp.dot(k_ref[0, pl.ds(key_off, rows), :], wq_ref[c], preferred_element_type=F32)
        if visible_below is not None:
            delta = (lax.broadcasted_iota(jnp.int32, (rows, blk), 0)
                     - lax.broadcasted_iota(jnp.int32, (rows, blk), 1))
            s = jnp.where(delta <= visible_below, s, NEG_INF)
        s_ref[c] = s
        smax_ref[c] = jnp.max(s, axis=0, keepdims=True)

    def consume(s_ref, key_off, rows, c):
        m_old = m_ref[c]
        m_new = jnp.maximum(m_old, smax_ref[c])
        p = jnp.exp2(s_ref[c] - m_new).astype(BF16)
        corr = jnp.exp2(m_old - m_new)
        vt = vt_ref[0, c // 2, :, pl.ds(key_off, rows)]
        pv = jnp.dot(vt, p, preferred_element_type=F32)
        acc_ref[c] = acc_ref[c] * corr + pv
        m_ref[c] = m_new

    def phase(s_ref, rows, first_key, n_steps, ahead, masked, unroll):
        def key_off(t):
            return pl.multiple_of(first_key + t * rows, rows)

        def bound(t):
            return (i * blk - key_off(t)) if masked else None

        def step(t, prefetch_next_step):
            for c in range(N_COMBO):
                nxt = c + ahead
                if nxt < N_COMBO:
                    scores(s_ref, key_off(t), rows, nxt, bound(t))
                elif prefetch_next_step:
                    scores(s_ref, key_off(t + 1), rows, nxt - N_COMBO, bound(t + 1))
                consume(s_ref, key_off(t), rows, c)

        for c in range(ahead):
            scores(s_ref, key_off(0), rows, c, bound(0))

        n_pre = n_steps - 1

        def body(tt, carry):
            for k in range(unroll):
                step(tt * unroll + k, True)
            return carry

        lax.fori_loop(0, n_pre // unroll, body, 0)
        if unroll > 1:
            def tail(t, carry):
                step(t, True)
                return carry

            lax.fori_loop((n_pre // unroll) * unroll, n_pre, tail, 0)
        step(n_steps - 1, False)

    per_big = big // blk
    n_big = i // per_big

    @pl.when(n_big > 0)
    def _():
        phase(sbig_ref, big, 0, n_big, AHEAD_BIG, False, 4)

    phase(ssml_ref, blk, n_big * big, i - n_big * per_big + 1, AHEAD, True, 1)

    lam = (jnp.exp(jnp.sum(lq1_ref[...] * lk1_ref[...], keepdims=True))
           - jnp.exp(jnp.sum(lq2_ref[...] * lk2_ref[...], keepdims=True))
           + lambda_init)
    for h in range(ATTN_HEADS):
        a0 = acc_ref[2 * h]
        a1 = acc_ref[2 * h + 1]
        o0 = a0[0:ATTN_V_DIM] / a0[ATTN_V_DIM:ATTN_V_DIM + 1]
        o1 = a1[0:ATTN_V_DIM] / a1[ATTN_V_DIM:ATTN_V_DIM + 1]
        o = o0 - lam * o1
        ms = jnp.mean(o * o, axis=0, keepdims=True)
        y = o * lax.rsqrt(ms + NORM_EPS) * gsub_ref[...]
        o_ref[0, h * ATTN_V_DIM:(h + 1) * ATTN_V_DIM, :] = y.astype(BF16)


def _attention(l, lq1, lk1, lq2, lk2, gsub, qt, k, vt, *, blk, big, lambda_init):
    B, S, _ = k.shape
    small = lambda shape: _layer_spec(l, shape)
    return pl.pallas_call(
        functools.partial(_attn_kernel, blk=blk, big=big, lambda_init=lambda_init),
        grid=(B, S // blk),
        in_specs=[
            small((1, ATTN_QK_DIM)), small((1, ATTN_QK_DIM)),
            small((1, ATTN_QK_DIM)), small((1, ATTN_QK_DIM)),
            small((ATTN_V_DIM, blk)),
            pl.BlockSpec((1, GROUP, blk), lambda b, t: (b, 0, t)),
            pl.BlockSpec((1, S, GROUP), lambda b, t: (b, 0, 0), pipeline_mode=pl.Buffered(1)),
            pl.BlockSpec((1, ATTN_HEADS, V_ROWS, S), lambda b, t: (b, 0, 0, 0),
                         pipeline_mode=pl.Buffered(1)),
        ],
        out_specs=pl.BlockSpec((1, GROUP, blk), lambda b, t: (b, 0, t)),
        out_shape=jax.ShapeDtypeStruct((B, GROUP, S), BF16),
        scratch_shapes=[
            pltpu.VMEM((N_COMBO, GROUP, blk), BF16),
            pltpu.VMEM((N_COMBO, 1, blk), F32),
            pltpu.VMEM((N_COMBO, V_ROWS, blk), F32),
            pltpu.VMEM((N_COMBO, big, blk), F32),
            pltpu.VMEM((N_COMBO, blk, blk), F32),
            pltpu.VMEM((N_COMBO, 1, blk), F32),
        ],
        compiler_params=_params(("arbitrary", "arbitrary")),
        name="attn",
    )(lq1, lk1, lq2, lk2, gsub, qt, k, vt)


def _linear_scan(a, u):
    rows = a.shape[0]
    row = lax.broadcasted_iota(jnp.int32, (rows, 1), 0)
    d = 1
    while d < rows:
        keep = row >= d
        a_prev = pltpu.roll(a, d, 0)
        u_prev = pltpu.roll(u, d, 0)
        u = jnp.where(keep, a * u_prev + u, u)
        a = jnp.where(keep, a * a_prev, a)
        d *= 2
    return a, u


def _causal_taps(cur, prev_ref, w_ref, taps):
    halo = prev_ref.shape[0]
    tm = cur.shape[0]
    assert halo % 8 == 0 and halo >= taps - 1
    ext = jnp.concatenate([prev_ref[...], cur], axis=0)
    prev_ref[...] = cur[tm - halo:tm]
    acc = None
    for r in range(min(8, taps)):
        rot = ext if r == 0 else pltpu.roll(ext, r, 0)
        for a in range((taps - 1 - r) // 8 + 1):
            k = taps - 1 - (8 * a + r)
            term = w_ref[k:k + 1, :] * rot[halo - 8 * a:halo - 8 * a + tm]
            acc = term if acc is None else acc + term
    return acc


def _mixer_kernel(h_ref, rest_ref, ybt_ref, lcw_ref, lcb_ref, wgate_ref, bgate_ref, lam_ref,
                  cw_ref, cb_ref, lng_ref, lnb_ref, scw_ref, wout_ref, gpost_ref,
                  out_ref, axprev_ref, cprev_ref, dprev_ref, hstate_ref, *, tm):
    G = GROUP

    @pl.when(pl.program_id(1) == 0)
    def _():
        axprev_ref[...] = jnp.zeros(axprev_ref.shape, F32)
        cprev_ref[...] = jnp.zeros(cprev_ref.shape, F32)
        dprev_ref[...] = jnp.zeros(dprev_ref.shape, F32)
        hstate_ref[...] = jnp.zeros(hstate_ref.shape, F32)

    xa = _causal_taps(rest_ref[0, :, 0:G], axprev_ref, lcw_ref, LRU_CONV) + lcb_ref[...]
    gates = jnp.dot(xa.astype(BF16), wgate_ref[...], preferred_element_type=F32) + bgate_ref[...]
    r = jax.nn.sigmoid(gates[:, 0:G])
    ig = jax.nn.sigmoid(gates[:, G:2 * G])
    neg_lam = -lam_ref[...]
    softplus = jnp.maximum(neg_lam, 0.0) + jnp.log1p(jnp.exp(-jnp.abs(neg_lam)))
    log_a = (-LRU_C) * r * softplus
    a = jnp.exp(log_a)
    u = jnp.sqrt(1.0 - a * a) * (ig * xa)
    a_cum, h_loc = _linear_scan(a, u)
    hh = h_loc + a_cum * hstate_ref[0:1, :]
    hstate_ref[0:1, :] = hh[tm - 1:tm]
    ya = _gelu_tanh_times(rest_ref[0, :, G:2 * G], hh)

    c = rest_ref[0, :, 2 * G:3 * G] * jax.nn.sigmoid(rest_ref[0, :, 3 * G:4 * G])
    cc = _causal_taps(c, cprev_ref, cw_ref, CONF_KERNEL) + cb_ref[...]
    mu = jnp.mean(cc, axis=-1, keepdims=True)
    xc = cc - mu
    var = jnp.mean(xc * xc, axis=-1, keepdims=True)
    ln = xc * lax.rsqrt(var + LN_EPS) * lng_ref[...] + lnb_ref[...]
    yc = ln * jax.nn.sigmoid(ln)

    dd = rest_ref[0, :, 6 * G:7 * G] * rest_ref[0, :, 4 * G:5 * G]
    yd = rest_ref[0, :, 5 * G:6 * G] * _causal_taps(dd, dprev_ref, scw_ref, SC_KERNEL)

    m = jnp.dot(ya.astype(BF16), wout_ref[0:G, :], preferred_element_type=F32)
    m += lax.dot_general(ybt_ref[0], wout_ref[G:2 * G, :], TN_DIMS, preferred_element_type=F32)
    m += jnp.dot(yc.astype(BF16), wout_ref[2 * G:3 * G, :], preferred_element_type=F32)
    m += jnp.dot(yd.astype(BF16), wout_ref[3 * G:4 * G, :], preferred_element_type=F32)
    out_ref[0] = h_ref[0] + _rms(m, gpost_ref[...])


def _mixer(l, h, rest, ybt, lcw, lcb, wgate, bgate, lam, cw, cb, lng, lnb, scw, wout, gpost, *, tm):
    B, S, D = h.shape
    G = GROUP
    return pl.pallas_call(
        functools.partial(_mixer_kernel, tm=tm),
        grid=(B, S // tm),
        in_specs=[
            pl.BlockSpec((1, tm, D), lambda b, t: (b, t, 0)),
            pl.BlockSpec((1, tm, REST_W), lambda b, t: (b, t, 0)),
            pl.BlockSpec((1, G, tm), lambda b, t: (b, 0, t)),
            _layer_spec(l, (LRU_CONV, G)), _layer_spec(l, (1, G)),
            _layer_spec(l, (G, 2 * G)), _layer_spec(l, (1, 2 * G)), _layer_spec(l, (1, G)),
            _layer_spec(l, (CONF_KERNEL, G)), _layer_spec(l, (1, G)),
            _layer_spec(l, (1, G)), _layer_spec(l, (1, G)),
            _layer_spec(l, (SC_KERNEL, G)),
            _layer_spec(l, (D, D)), _layer_spec(l, (1, D)),
        ],
        out_specs=pl.BlockSpec((1, tm, D), lambda b, t: (b, t, 0)),
        out_shape=jax.ShapeDtypeStruct((B, S, D), F32),
        scratch_shapes=[
            pltpu.VMEM((8, G), F32),
            pltpu.VMEM((32, G), F32),
            pltpu.VMEM((8, G), F32),
            pltpu.VMEM((8, G), F32),
        ],
        compiler_params=_params(("arbitrary", "arbitrary")),
        name="mixer",
    )(h, rest, ybt, lcw, lcb, wgate, bgate, lam, cw, cb, lng, lnb, scw, wout, gpost)


def _ffn_kernel(h_ref, gpre_ref, wup_ref, cw_ref, cb_ref, wdown_ref, gpost_ref,
                out_ref, gprev_ref, *, tm, chunk):
    @pl.when(pl.program_id(1) == 0)
    def _():
        gprev_ref[...] = jnp.zeros(gprev_ref.shape, F32)

    x = h_ref[0]
    yb = _rms(x, gpre_ref[...]).astype(BF16)
    n_chunks = D_FF // chunk

    def up(c):
        lo = c * chunk
        u = jnp.dot(yb, wup_ref[:, lo:lo + chunk], preferred_element_type=F32)
        g = jnp.dot(yb, wup_ref[:, D_FF + lo:D_FF + lo + chunk], preferred_element_type=F32)
        return u, g

    f = jnp.zeros((tm, D_MODEL), F32)
    pending = [up(c) for c in range(FFN_AHEAD)]
    for c in range(n_chunks):
        lo = c * chunk
        u, g = pending.pop(0)
        if c + FFN_AHEAD < n_chunks:
            pending.append(up(c + FFN_AHEAD))
        gc = _causal_taps(g, gprev_ref.at[c], cw_ref.at[:, lo:lo + chunk], FFN_KERNEL)
        gc = gc + cb_ref[:, lo:lo + chunk]
        act = (_gelu_tanh_times(gc, u)).astype(BF16)
        f += jnp.dot(act, wdown_ref[lo:lo + chunk, :], preferred_element_type=F32)
    out_ref[0] = x + _rms(f, gpost_ref[...])


def _ffn(l, h, gpre, wup, cw, cb, wdown, gpost, *, tm, chunk):
    B, S, D = h.shape
    return pl.pallas_call(
        functools.partial(_ffn_kernel, tm=tm, chunk=chunk),
        grid=(B, S // tm),
        in_specs=[
            pl.BlockSpec((1, tm, D), lambda b, t: (b, t, 0)),
            _layer_spec(l, (1, D)),
            _layer_spec(l, (D, 2 * D_FF)),
            _layer_spec(l, (FFN_KERNEL, D_FF)), _layer_spec(l, (1, D_FF)),
            _layer_spec(l, (D_FF, D)), _layer_spec(l, (1, D)),
        ],
        out_specs=pl.BlockSpec((1, tm, D), lambda b, t: (b, t, 0)),
        out_shape=jax.ShapeDtypeStruct((B, S, D), F32),
        scratch_shapes=[pltpu.VMEM((D_FF // chunk, 8, chunk), F32)],
        compiler_params=_params(("arbitrary", "arbitrary")),
        name="ffn",
    )(h, gpre, wup, cw, cb, wdown, gpost)


def _block_diag(w):
    heads, hd, _ = w.shape
    eye = jnp.eye(heads, dtype=w.dtype)
    return (eye[:, None, :, None] * w[:, :, None, :]).reshape(heads * hd, heads * hd)


def _pick_tile(S, want):
    t = min(want, S)
    while S % t:
        t //= 2
    return t


def kernel(x, norm_mix_pre, norm_mix_post, norm_ffn_pre, norm_ffn_post, w_in, lru_conv_w, lru_conv_b, lru_wa, lru_ba, lru_wx, lru_bx, lru_lambda, attn_lq1, attn_lk1, attn_lq2, attn_lk2, attn_subln, conf_dw_w, conf_dw_b, conf_ln_g, conf_ln_b, sc_conv_w, w_out, ffn_w_up, ffn_conv_w, ffn_conv_b, ffn_w_down):
    B, S, D = x.shape
    assert D == D_MODEL
    depth = w_in.shape[0]
    tm = _pick_tile(S, 512)
    blk = _pick_tile(S, 256)
    big = _pick_tile(S, 1024)
    assert tm >= 32 and blk % 128 == 0 and big % blk == 0
    layers = lambda v: v.reshape(depth, 1, -1)
    bf16 = lambda v: v.astype(BF16)

    w_in_b, w_out_b, w_up_b, w_down_b = bf16(w_in), bf16(w_out), bf16(ffn_w_up), bf16(ffn_w_down)
    wgate = bf16(jnp.concatenate([jax.vmap(_block_diag)(lru_wa), jax.vmap(_block_diag)(lru_wx)],
                                 axis=2))
    bgate = layers(jnp.concatenate([lru_ba, lru_bx], axis=1))
    lambda_init = [0.8 - 0.6 * math.exp(-0.3 * l) for l in range(depth)]
    gsub = jnp.stack([jnp.broadcast_to((attn_subln[l] * (1.0 - lambda_init[l]))[:, None],
                                       (ATTN_V_DIM, blk)) for l in range(depth)])
    g_mix_pre, g_mix_post = layers(norm_mix_pre), layers(norm_mix_post)
    g_ffn_pre, g_ffn_post = layers(norm_ffn_pre), layers(norm_ffn_post)
    lq1, lk1, lq2, lk2 = layers(attn_lq1), layers(attn_lk1), layers(attn_lq2), layers(attn_lk2)
    lcb, lam, cb = layers(lru_conv_b), layers(lru_lambda), layers(conf_dw_b)
    lng, lnb, fcb = layers(conf_ln_g), layers(conf_ln_b), layers(ffn_conv_b)

    h = x
    for l in range(depth):
        rest, k, qt, vt = _inproj(l, h, g_mix_pre, w_in_b, tm=tm)
        ybt = _attention(l, lq1, lk1, lq2, lk2, gsub, qt, k, vt,
                         blk=blk, big=big, lambda_init=lambda_init[l])
        h = _mixer(l, h, rest, ybt, lru_conv_w, lcb, wgate, bgate, lam, conf_dw_w, cb, lng, lnb,
                   sc_conv_w, w_out_b, g_mix_post, tm=tm)
        h = _ffn(l, h, g_ffn_pre, w_up_b, ffn_conv_w, fcb, w_down_b, g_ffn_post,
                 tm=tm, chunk=256)
    return h
```

```python
import functools
import math

import jax
import jax.numpy as jnp
from jax import lax
from jax.experimental import pallas as pl
from jax.experimental.pallas import tpu as pltpu

F32 = jnp.float32
BF16 = jnp.bfloat16

D_MODEL = 1024
GROUP = D_MODEL // 4
LRU_HEADS = 4
LRU_HEAD_DIM = GROUP // LRU_HEADS
LRU_CONV = 4
LRU_C = 8.0
ATTN_HEADS = 4
ATTN_V_DIM = GROUP // ATTN_HEADS
ATTN_QK_DIM = ATTN_V_DIM // 2
CONF_KERNEL = 31
SC_KERNEL = 3
D_FF = 11 * D_MODEL // 4
FFN_KERNEL = 3
NORM_EPS = 1e-6
LN_EPS = 1e-5
NEG_INF = -1e30

N_COMBO = 2 * ATTN_HEADS
AHEAD = 5
AHEAD_BIG = 6
V_ROWS = ATTN_V_DIM + 16
REST_W = 7 * GROUP
LOG2E = 1.4426950408889634
VMEM_LIMIT_BYTES = 56 * 1024 * 1024

TN_DIMS = (((0,), (0,)), ((), ()))


def _rms(x, g):
    return x * lax.rsqrt(jnp.mean(x * x, axis=-1, keepdims=True) + NORM_EPS) * g


def _gelu_tanh_times(x, y):
    c0 = math.sqrt(2.0 / math.pi)
    inner = x * (c0 + (c0 * 0.044715) * (x * x))
    return (x * y) * (0.5 * jnp.tanh(inner) + 0.5)


def _layer_spec(l, shape):
    nd = len(shape)
    return pl.BlockSpec((None,) + tuple(shape), lambda b, t: (l,) + (0,) * nd,
                        pipeline_mode=pl.Buffered(1))


def _params(sem):
    return pltpu.CompilerParams(dimension_semantics=sem, vmem_limit_bytes=VMEM_LIMIT_BYTES)


def _inproj_kernel(h_ref, g_ref, w_ref, rest_ref, k_ref, qt_ref, vt_ref, *, tm):
    G = GROUP
    yb = _rms(h_ref[0], g_ref[...]).astype(BF16)

    def proj(lo, hi):
        return jnp.dot(yb, w_ref[:, lo:hi], preferred_element_type=F32)

    rest_ref[0, :, 0:2 * G] = proj(0, 2 * G)
    rest_ref[0, :, 2 * G:REST_W] = proj(5 * G, 10 * G)
    k_ref[0] = proj(3 * G, 4 * G).astype(BF16)
    q = proj(2 * G, 3 * G) * (ATTN_QK_DIM ** -0.5 * LOG2E)
    qt_ref[0] = q.astype(BF16).T
    vt = proj(4 * G, 5 * G).astype(BF16).T
    ones = jnp.ones((V_ROWS - ATTN_V_DIM, tm), BF16)
    for h in range(ATTN_HEADS):
        vt_ref[0, h, 0:ATTN_V_DIM, :] = vt[h * ATTN_V_DIM:(h + 1) * ATTN_V_DIM]
        vt_ref[0, h, ATTN_V_DIM:V_ROWS, :] = ones


def _inproj(l, h, g, w_in, *, tm):
    B, S, D = h.shape
    return pl.pallas_call(
        functools.partial(_inproj_kernel, tm=tm),
        grid=(B, S // tm),
        in_specs=[
            pl.BlockSpec((1, tm, D), lambda b, t: (b, t, 0)),
            _layer_spec(l, (1, D)),
            _layer_spec(l, (D, 10 * GROUP)),
        ],
        out_specs=[
            pl.BlockSpec((1, tm, REST_W), lambda b, t: (b, t, 0)),
            pl.BlockSpec((1, tm, GROUP), lambda b, t: (b, t, 0)),
            pl.BlockSpec((1, GROUP, tm), lambda b, t: (b, 0, t)),
            pl.BlockSpec((1, ATTN_HEADS, V_ROWS, tm), lambda b, t: (b, 0, 0, t)),
        ],
        out_shape=[
            jax.ShapeDtypeStruct((B, S, REST_W), F32),
            jax.ShapeDtypeStruct((B, S, GROUP), BF16),
            jax.ShapeDtypeStruct((B, GROUP, S), BF16),
            jax.ShapeDtypeStruct((B, ATTN_HEADS, V_ROWS, S), BF16),
        ],
        compiler_params=_params(("arbitrary", "arbitrary")),
        name="inproj",
    )(h, g, w_in)


def _attn_kernel(lq1_ref, lk1_ref, lq2_ref, lk2_ref, gsub_ref, qt_ref, k_ref, vt_ref,
                 o_ref, wq_ref, m_ref, acc_ref, sbig_ref, ssml_ref, smax_ref,
                 *, blk, big, lambda_init):
    i = pl.program_id(1)

    @pl.when(i == 0)
    def _():
        wq_ref[...] = jnp.zeros(wq_ref.shape, BF16)

    for c in range(N_COMBO):
        rows = slice(c * ATTN_QK_DIM, (c + 1) * ATTN_QK_DIM)
        wq_ref[c, rows, :] = qt_ref[0, rows, :]
    m_ref[...] = jnp.full(m_ref.shape, NEG_INF, F32)
    acc_ref[...] = jnp.zeros(acc_ref.shape, F32)

    def scores(s_ref, key_off, rows, c, visible_below):
        s = jnp.dot(k_ref[0, pl.ds(key_off, rows), :], wq_ref[c], preferred_element_type=F32)
        if visible_below is not None:
            delta = (lax.broadcasted_iota(jnp.int32, (rows, blk), 0)
                     - lax.broadcasted_iota(jnp.int32, (rows, blk), 1))
            s = jnp.where(delta <= visible_below, s, NEG_INF)
        s_ref[c, 0:rows, :] = s
        smax_ref[c] = jnp.max(s, axis=0, keepdims=True)

    def consume(s_ref, key_off, rows, c):
        m_old = m_ref[c]
        m_new = jnp.maximum(m_old, smax_ref[c])
        p = jnp.exp2(s_ref[c, 0:rows, :] - m_new).astype(BF16)
        corr = jnp.exp2(m_old - m_new)
        vt = vt_ref[0, c // 2, :, pl.ds(key_off, rows)]
        pv = jnp.dot(vt, p, preferred_element_type=F32)
        acc_ref[c] = acc_ref[c] * corr + pv
        m_ref[c] = m_new

    def phase(s_ref, rows, first_key, n_steps, ahead, masked, unroll):
        def key_off(t):
            return pl.multiple_of(first_key + t * rows, rows)

        def bound(t):
            return (i * blk - key_off(t)) if masked else None

        def step(t, prefetch_next_step):
            for c in range(N_COMBO):
                nxt = c + ahead
                if nxt < N_COMBO:
                    scores(s_ref, key_off(t), rows, nxt, bound(t))
                elif prefetch_next_step:
                    scores(s_ref, key_off(t + 1), rows, nxt - N_COMBO, bound(t + 1))
                consume(s_ref, key_off(t), rows, c)

        for c in range(ahead):
            scores(s_ref, key_off(0), rows, c, bound(0))

        n_pre = n_steps - 1

        def body(tt, carry):
            for k in range(unroll):
                step(tt * unroll + k, True)
            return carry

        lax.fori_loop(0, n_pre // unroll, body, 0)
        if unroll > 1:
            def tail(t, carry):
                step(t, True)
                return carry

            lax.fori_loop((n_pre // unroll) * unroll, n_pre, tail, 0)
        step(n_steps - 1, False)

    per_big = big // blk
    n_big = i // per_big

    @pl.when(n_big > 0)
    def _():
        phase(sbig_ref, big, 0, n_big, AHEAD_BIG, False, 4)

    phase(ssml_ref, blk, n_big * big, i - n_big * per_big + 1, AHEAD, True, 1)

    lam = (jnp.exp(jnp.sum(lq1_ref[...] * lk1_ref[...], keepdims=True))
           - jnp.exp(jnp.sum(lq2_ref[...] * lk2_ref[...], keepdims=True))
           + lambda_init)
    for h in range(ATTN_HEADS):
        a0 = acc_ref[2 * h]
        a1 = acc_ref[2 * h + 1]
        o0 = a0[0:ATTN_V_DIM] / a0[ATTN_V_DIM:ATTN_V_DIM + 1]
        o1 = a1[0:ATTN_V_DIM] / a1[ATTN_V_DIM:ATTN_V_DIM + 1]
        o = o0 - lam * o1
        ms = jnp.mean(o * o, axis=0, keepdims=True)
        y = o * lax.rsqrt(ms + NORM_EPS) * gsub_ref[...]
        o_ref[0, h * ATTN_V_DIM:(h + 1) * ATTN_V_DIM, :] = y.astype(BF16)


def _attention(l, lq1, lk1, lq2, lk2, gsub, qt, k, vt, *, blk, big, lambda_init):
    B, S, _ = k.shape
    small = lambda shape: _layer_spec(l, shape)
    return pl.pallas_call(
        functools.partial(_attn_kernel, blk=blk, big=big, lambda_init=lambda_init),
        grid=(B, S // blk),
        in_specs=[
            small((1, ATTN_QK_DIM)), small((1, ATTN_QK_DIM)),
            small((1, ATTN_QK_DIM)), small((1, ATTN_QK_DIM)),
            small((ATTN_V_DIM, blk)),
            pl.BlockSpec((1, GROUP, blk), lambda b, t: (b, 0, t)),
            pl.BlockSpec((1, S, GROUP), lambda b, t: (b, 0, 0), pipeline_mode=pl.Buffered(1)),
            pl.BlockSpec((1, ATTN_HEADS, V_ROWS, S), lambda b, t: (b, 0, 0, 0),
                         pipeline_mode=pl.Buffered(1)),
        ],
        out_specs=pl.BlockSpec((1, GROUP, blk), lambda b, t: (b, 0, t)),
        out_shape=jax.ShapeDtypeStruct((B, GROUP, S), BF16),
        scratch_shapes=[
            pltpu.VMEM((N_COMBO, GROUP, blk), BF16),
            pltpu.VMEM((N_COMBO, 1, blk), F32),
            pltpu.VMEM((N_COMBO, V_ROWS, blk), F32),
            pltpu.VMEM((N_COMBO, big + 8, blk), F32),
            pltpu.VMEM((N_COMBO, blk, blk), F32),
            pltpu.VMEM((N_COMBO, 1, blk), F32),
        ],
        compiler_params=_params(("arbitrary", "arbitrary")),
        name="attn",
    )(lq1, lk1, lq2, lk2, gsub, qt, k, vt)


def _linear_scan(a, u):
    rows = a.shape[0]
    row = lax.broadcasted_iota(jnp.int32, (rows, 1), 0)
    d = 1
    while d < rows:
        keep = row >= d
        a_prev = pltpu.roll(a, d, 0)
        u_prev = pltpu.roll(u, d, 0)
        u = jnp.where(keep, a * u_prev + u, u)
        a = jnp.where(keep, a * a_prev, a)
        d *= 2
    return a, u


def _causal_taps(cur, prev_ref, w_ref, taps):
    halo = prev_ref.shape[0]
    tm = cur.shape[0]
    assert halo % 8 == 0 and halo >= taps - 1
    ext = jnp.concatenate([prev_ref[...], cur], axis=0)
    prev_ref[...] = cur[tm - halo:tm]
    acc = None
    for r in range(min(8, taps)):
        rot = ext if r == 0 else pltpu.roll(ext, r, 0)
        for a in range((taps - 1 - r) // 8 + 1):
            k = taps - 1 - (8 * a + r)
            term = w_ref[k:k + 1, :] * rot[halo - 8 * a:halo - 8 * a + tm]
            acc = term if acc is None else acc + term
    return acc


def _mixer_kernel(h_ref, rest_ref, ybt_ref, lcw_ref, lcb_ref, wgate_ref, bgate_ref, lam_ref,
                  cw_ref, cb_ref, lng_ref, lnb_ref, scw_ref, wout_ref, gpost_ref,
                  out_ref, axprev_ref, cprev_ref, dprev_ref, hstate_ref, *, tm):
    G = GROUP

    @pl.when(pl.program_id(1) == 0)
    def _():
        axprev_ref[...] = jnp.zeros(axprev_ref.shape, F32)
        cprev_ref[...] = jnp.zeros(cprev_ref.shape, F32)
        dprev_ref[...] = jnp.zeros(dprev_ref.shape, F32)
        hstate_ref[...] = jnp.zeros(hstate_ref.shape, F32)

    xa = _causal_taps(rest_ref[0, :, 0:G], axprev_ref, lcw_ref, LRU_CONV) + lcb_ref[...]
    gates = jnp.dot(xa.astype(BF16), wgate_ref[...], preferred_element_type=F32) + bgate_ref[...]
    r = jax.nn.sigmoid(gates[:, 0:G])
    ig = jax.nn.sigmoid(gates[:, G:2 * G])
    neg_lam = -lam_ref[...]
    softplus = jnp.maximum(neg_lam, 0.0) + jnp.log1p(jnp.exp(-jnp.abs(neg_lam)))
    log_a = (-LRU_C) * r * softplus
    a = jnp.exp(log_a)
    u = jnp.sqrt(1.0 - a * a) * (ig * xa)
    a_cum, h_loc = _linear_scan(a, u)
    hh = h_loc + a_cum * hstate_ref[0:1, :]
    hstate_ref[0:1, :] = hh[tm - 1:tm]
    ya = _gelu_tanh_times(rest_ref[0, :, G:2 * G], hh)

    c = rest_ref[0, :, 2 * G:3 * G] * jax.nn.sigmoid(rest_ref[0, :, 3 * G:4 * G])
    cc = _causal_taps(c, cprev_ref, cw_ref, CONF_KERNEL) + cb_ref[...]
    mu = jnp.mean(cc, axis=-1, keepdims=True)
    xc = cc - mu
    var = jnp.mean(xc * xc, axis=-1, keepdims=True)
    ln = xc * lax.rsqrt(var + LN_EPS) * lng_ref[...] + lnb_ref[...]
    yc = ln * jax.nn.sigmoid(ln)

    dd = rest_ref[0, :, 6 * G:7 * G] * rest_ref[0, :, 4 * G:5 * G]
    yd = rest_ref[0, :, 5 * G:6 * G] * _causal_taps(dd, dprev_ref, scw_ref, SC_KERNEL)

    m = jnp.dot(ya.astype(BF16), wout_ref[0:G, :], preferred_element_type=F32)
    m += lax.dot_general(ybt_ref[0], wout_ref[G:2 * G, :], TN_DIMS, preferred_element_type=F32)
    m += jnp.dot(yc.astype(BF16), wout_ref[2 * G:3 * G, :], preferred_element_type=F32)
    m += jnp.dot(yd.astype(BF16), wout_ref[3 * G:4 * G, :], preferred_element_type=F32)
    out_ref[0] = h_ref[0] + _rms(m, gpost_ref[...])


def _mixer(l, h, rest, ybt, lcw, lcb, wgate, bgate, lam, cw, cb, lng, lnb, scw, wout, gpost, *, tm):
    B, S, D = h.shape
    G = GROUP
    return pl.pallas_call(
        functools.partial(_mixer_kernel, tm=tm),
        grid=(B, S // tm),
        in_specs=[
            pl.BlockSpec((1, tm, D), lambda b, t: (b, t, 0)),
            pl.BlockSpec((1, tm, REST_W), lambda b, t: (b, t, 0)),
            pl.BlockSpec((1, G, tm), lambda b, t: (b, 0, t)),
            _layer_spec(l, (LRU_CONV, G)), _layer_spec(l, (1, G)),
            _layer_spec(l, (G, 2 * G)), _layer_spec(l, (1, 2 * G)), _layer_spec(l, (1, G)),
            _layer_spec(l, (CONF_KERNEL, G)), _layer_spec(l, (1, G)),
            _layer_spec(l, (1, G)), _layer_spec(l, (1, G)),
            _layer_spec(l, (SC_KERNEL, G)),
            _layer_spec(l, (D, D)), _layer_spec(l, (1, D)),
        ],
        out_specs=pl.BlockSpec((1, tm, D), lambda b, t: (b, t, 0)),
        out_shape=jax.ShapeDtypeStruct((B, S, D), F32),
        scratch_shapes=[
            pltpu.VMEM((8, G), F32),
            pltpu.VMEM((32, G), F32),
            pltpu.VMEM((8, G), F32),
            pltpu.VMEM((8, G), F32),
        ],
        compiler_params=_params(("arbitrary", "arbitrary")),
        name="mixer",
    )(h, rest, ybt, lcw, lcb, wgate, bgate, lam, cw, cb, lng, lnb, scw, wout, gpost)


def _ffn_kernel(h_ref, gpre_ref, wup_ref, cw_ref, cb_ref, wdown_ref, gpost_ref,
                out_ref, gprev_ref, *, tm, chunk):
    @pl.when(pl.program_id(1) == 0)
    def _():
        gprev_ref[...] = jnp.zeros(gprev_ref.shape, F32)

    x = h_ref[0]
    yb = _rms(x, gpre_ref[...]).astype(BF16)
    n_chunks = D_FF // chunk

    def up(c):
        lo = c * chunk
        u = jnp.dot(yb, wup_ref[:, lo:lo + chunk], preferred_element_type=F32)
        g = jnp.dot(yb, wup_ref[:, D_FF + lo:D_FF + lo + chunk], preferred_element_type=F32)
        return u, g

    f = jnp.zeros((tm, D_MODEL), F32)
    ups = [up(c) for c in range(n_chunks)]
    for c in range(n_chunks):
        lo = c * chunk
        u, g = ups[c]
        gc = _causal_taps(g, gprev_ref.at[c], cw_ref.at[:, lo:lo + chunk], FFN_KERNEL)
        gc = gc + cb_ref[:, lo:lo + chunk]
        act = (_gelu_tanh_times(gc, u)).astype(BF16)
        f += jnp.dot(act, wdown_ref[lo:lo + chunk, :], preferred_element_type=F32)
    out_ref[0] = x + _rms(f, gpost_ref[...])


def _ffn(l, h, gpre, wup, cw, cb, wdown, gpost, *, tm, chunk):
    B, S, D = h.shape
    return pl.pallas_call(
        functools.partial(_ffn_kernel, tm=tm, chunk=chunk),
        grid=(B, S // tm),
        in_specs=[
            pl.BlockSpec((1, tm, D), lambda b, t: (b, t, 0)),
            _layer_spec(l, (1, D)),
            _layer_spec(l, (D, 2 * D_FF)),
            _layer_spec(l, (FFN_KERNEL, D_FF)), _layer_spec(l, (1, D_FF)),
            _layer_spec(l, (D_FF, D)), _layer_spec(l, (1, D)),
        ],
        out_specs=pl.BlockSpec((1, tm, D), lambda b, t: (b, t, 0)),
        out_shape=jax.ShapeDtypeStruct((B, S, D), F32),
        scratch_shapes=[pltpu.VMEM((D_FF // chunk, 8, chunk), F32)],
        compiler_params=_params(("arbitrary", "arbitrary")),
        name="ffn",
    )(h, gpre, wup, cw, cb, wdown, gpost)


def _block_diag(w):
    heads, hd, _ = w.shape
    eye = jnp.eye(heads, dtype=w.dtype)
    return (eye[:, None, :, None] * w[:, :, None, :]).reshape(heads * hd, heads * hd)


def _pick_tile(S, want):
    t = min(want, S)
    while S % t:
        t //= 2
    return t


def kernel(x, norm_mix_pre, norm_mix_post, norm_ffn_pre, norm_ffn_post, w_in, lru_conv_w, lru_conv_b, lru_wa, lru_ba, lru_wx, lru_bx, lru_lambda, attn_lq1, attn_lk1, attn_lq2, attn_lk2, attn_subln, conf_dw_w, conf_dw_b, conf_ln_g, conf_ln_b, sc_conv_w, w_out, ffn_w_up, ffn_conv_w, ffn_conv_b, ffn_w_down):
    B, S, D = x.shape
    assert D == D_MODEL
    depth = w_in.shape[0]
    tm = _pick_tile(S, 512)
    blk = _pick_tile(S, 256)
    big = _pick_tile(S, 1024)
    assert tm >= 32 and blk % 128 == 0 and big % blk == 0
    layers = lambda v: v.reshape(depth, 1, -1)
    bf16 = lambda v: v.astype(BF16)

    w_in_b, w_out_b, w_up_b, w_down_b = bf16(w_in), bf16(w_out), bf16(ffn_w_up), bf16(ffn_w_down)
    wgate = bf16(jnp.concatenate([jax.vmap(_block_diag)(lru_wa), jax.vmap(_block_diag)(lru_wx)],
                                 axis=2))
    bgate = layers(jnp.concatenate([lru_ba, lru_bx], axis=1))
    lambda_init = [0.8 - 0.6 * math.exp(-0.3 * l) for l in range(depth)]
    gsub = jnp.stack([jnp.broadcast_to((attn_subln[l] * (1.0 - lambda_init[l]))[:, None],
                                       (ATTN_V_DIM, blk)) for l in range(depth)])
    g_mix_pre, g_mix_post = layers(norm_mix_pre), layers(norm_mix_post)
    g_ffn_pre, g_ffn_post = layers(norm_ffn_pre), layers(norm_ffn_post)
    lq1, lk1, lq2, lk2 = layers(attn_lq1), layers(attn_lk1), layers(attn_lq2), layers(attn_lk2)
    lcb, lam, cb = layers(lru_conv_b), layers(lru_lambda), layers(conf_dw_b)
    lng, lnb, fcb = layers(conf_ln_g), layers(conf_ln_b), layers(ffn_conv_b)

    h = x
    for l in range(depth):
        rest, k, qt, vt = _inproj(l, h, g_mix_pre, w_in_b, tm=tm)
        ybt = _attention(l, lq1, lk1, lq2, lk2, gsub, qt, k, vt,
                         blk=blk, big=big, lambda_init=lambda_init[l])
        h = _mixer(l, h, rest, ybt, lru_conv_w, lcb, wgate, bgate, lam, conf_dw_w, cb, lng, lnb,
                   sc_conv_w, w_out_b, g_mix_post, tm=tm)
        h = _ffn(l, h, g_ffn_pre, w_up_b, ffn_conv_w, fcb, w_down_b, g_ffn_post,
                 tm=tm, chunk=256)
    return h
```

```python
import functools
import math

import jax
import jax.numpy as jnp
from jax import lax
from jax.experimental import pallas as pl
from jax.experimental.pallas import tpu as pltpu

F32 = jnp.float32
BF16 = jnp.bfloat16

D_MODEL = 1024
GROUP = D_MODEL // 4
LRU_HEADS = 4
LRU_HEAD_DIM = GROUP // LRU_HEADS
LRU_CONV = 4
LRU_C = 8.0
ATTN_HEADS = 4
ATTN_V_DIM = GROUP // ATTN_HEADS
ATTN_QK_DIM = ATTN_V_DIM // 2
CONF_KERNEL = 31
SC_KERNEL = 3
D_FF = 11 * D_MODEL // 4
FFN_KERNEL = 3
NORM_EPS = 1e-6
LN_EPS = 1e-5
NEG_INF = -1e30

N_COMBO = 2 * ATTN_HEADS
AHEAD = 5
AHEAD_BIG = 6
V_ROWS = ATTN_V_DIM + 16
REST_W = 7 * GROUP
LOG2E = 1.4426950408889634
VMEM_LIMIT_BYTES = 56 * 1024 * 1024

TN_DIMS = (((0,), (0,)), ((), ()))


def _rms(x, g):
    return x * lax.rsqrt(jnp.mean(x * x, axis=-1, keepdims=True) + NORM_EPS) * g


def _gelu_tanh_times(x, y):
    c0 = math.sqrt(2.0 / math.pi)
    inner = x * (c0 + (c0 * 0.044715) * (x * x))
    return (x * y) * (0.5 * jnp.tanh(inner) + 0.5)


def _layer_spec(l, shape):
    nd = len(shape)
    return pl.BlockSpec((None,) + tuple(shape), lambda b, t: (l,) + (0,) * nd,
                        pipeline_mode=pl.Buffered(1))


def _params(sem):
    return pltpu.CompilerParams(dimension_semantics=sem, vmem_limit_bytes=VMEM_LIMIT_BYTES)


def _inproj_kernel(h_ref, g_ref, w_ref, rest_ref, k_ref, qt_ref, vt_ref, *, tm):
    G = GROUP
    yb = _rms(h_ref[0], g_ref[...]).astype(BF16)

    def proj(lo, hi):
        return jnp.dot(yb, w_ref[:, lo:hi], preferred_element_type=F32)

    rest_ref[0, :, 0:2 * G] = proj(0, 2 * G)
    rest_ref[0, :, 2 * G:REST_W] = proj(5 * G, 10 * G)
    k_ref[0] = proj(3 * G, 4 * G).astype(BF16)
    q = proj(2 * G, 3 * G) * (ATTN_QK_DIM ** -0.5 * LOG2E)
    qt_ref[0] = q.astype(BF16).T
    vt = proj(4 * G, 5 * G).astype(BF16).T
    ones = jnp.ones((V_ROWS - ATTN_V_DIM, tm), BF16)
    for h in range(ATTN_HEADS):
        vt_ref[0, h, 0:ATTN_V_DIM, :] = vt[h * ATTN_V_DIM:(h + 1) * ATTN_V_DIM]
        vt_ref[0, h, ATTN_V_DIM:V_ROWS, :] = ones


def _inproj(l, h, g, w_in, *, tm):
    B, S, D = h.shape
    return pl.pallas_call(
        functools.partial(_inproj_kernel, tm=tm),
        grid=(B, S // tm),
        in_specs=[
            pl.BlockSpec((1, tm, D), lambda b, t: (b, t, 0)),
            _layer_spec(l, (1, D)),
            _layer_spec(l, (D, 10 * GROUP)),
        ],
        out_specs=[
            pl.BlockSpec((1, tm, REST_W), lambda b, t: (b, t, 0)),
            pl.BlockSpec((1, tm, GROUP), lambda b, t: (b, t, 0)),
            pl.BlockSpec((1, GROUP, tm), lambda b, t: (b, 0, t)),
            pl.BlockSpec((1, ATTN_HEADS, V_ROWS, tm), lambda b, t: (b, 0, 0, t)),
        ],
        out_shape=[
            jax.ShapeDtypeStruct((B, S, REST_W), F32),
            jax.ShapeDtypeStruct((B, S, GROUP), BF16),
            jax.ShapeDtypeStruct((B, GROUP, S), BF16),
            jax.ShapeDtypeStruct((B, ATTN_HEADS, V_ROWS, S), BF16),
        ],
        compiler_params=_params(("arbitrary", "arbitrary")),
        name="inproj",
    )(h, g, w_in)


def _attn_kernel(lq1_ref, lk1_ref, lq2_ref, lk2_ref, gsub_ref, qt_ref, k_ref, vt_ref,
                 o_ref, wq_ref, m_ref, acc_ref, sbig_ref, ssml_ref, smax_ref,
                 *, blk, big, lambda_init):
    i = pl.program_id(1)

    @pl.when(i == 0)
    def _():
        wq_ref[...] = jnp.zeros(wq_ref.shape, BF16)

    for c in range(N_COMBO):
        rows = slice(c * ATTN_QK_DIM, (c + 1) * ATTN_QK_DIM)
        wq_ref[c, rows, :] = qt_ref[0, rows, :]
    m_ref[...] = jnp.full(m_ref.shape, NEG_INF, F32)
    acc_ref[...] = jnp.zeros(acc_ref.shape, F32)

    def scores(s_ref, key_off, rows, c, visible_below):
        s = jnp.dot(k_ref[0, pl.ds(key_off, rows), :], wq_ref[c], preferred_element_type=F32)
        if visible_below is not None:
            delta = (lax.broadcasted_iota(jnp.int32, (rows, blk), 0)
                     - lax.broadcasted_iota(jnp.int32, (rows, blk), 1))
            s = jnp.where(delta <= visible_below, s, NEG_INF)
        s_ref[c, 0:rows, :] = s
        smax_ref[c] = jnp.max(s, axis=0, keepdims=True)

    def consume(s_ref, key_off, rows, c):
        m_old = m_ref[c]
        m_new = jnp.maximum(m_old, smax_ref[c])
        p = jnp.exp2(s_ref[c, 0:rows, :] - m_new).astype(BF16)
        corr = jnp.exp2(m_old - m_new)
        vt = vt_ref[0, c // 2, :, pl.ds(key_off, rows)]
        pv = jnp.dot(vt, p, preferred_element_type=F32)
        acc_ref[c] = acc_ref[c] * corr + pv
        m_ref[c] = m_new

    def phase(s_ref, rows, first_key, n_steps, ahead, masked, unroll):
        def key_off(t):
            return pl.multiple_of(first_key + t * rows, rows)

        def bound(t):
            return (i * blk - key_off(t)) if masked else None

        def step(t, prefetch_next_step):
            for c in range(N_COMBO):
                nxt = c + ahead
                if nxt < N_COMBO:
                    scores(s_ref, key_off(t), rows, nxt, bound(t))
                elif prefetch_next_step:
                    scores(s_ref, key_off(t + 1), rows, nxt - N_COMBO, bound(t + 1))
                consume(s_ref, key_off(t), rows, c)

        for c in range(ahead):
            scores(s_ref, key_off(0), rows, c, bound(0))

        n_pre = n_steps - 1

        def body(tt, carry):
            for k in range(unroll):
                step(tt * unroll + k, True)
            return carry

        lax.fori_loop(0, n_pre // unroll, body, 0)
        if unroll > 1:
            def tail(t, carry):
                step(t, True)
                return carry

            lax.fori_loop((n_pre // unroll) * unroll, n_pre, tail, 0)
        step(n_steps - 1, False)

    per_big = big // blk
    n_big = i // per_big

    @pl.when(n_big > 0)
    def _():
        phase(sbig_ref, big, 0, n_big, AHEAD_BIG, False, 4)

    phase(ssml_ref, blk, n_big * big, i - n_big * per_big + 1, AHEAD, True, 1)

    lam = (jnp.exp(jnp.sum(lq1_ref[...] * lk1_ref[...], keepdims=True))
           - jnp.exp(jnp.sum(lq2_ref[...] * lk2_ref[...], keepdims=True))
           + lambda_init)
    for h in range(ATTN_HEADS):
        a0 = acc_ref[2 * h]
        a1 = acc_ref[2 * h + 1]
        o0 = a0[0:ATTN_V_DIM] / a0[ATTN_V_DIM:ATTN_V_DIM + 1]
        o1 = a1[0:ATTN_V_DIM] / a1[ATTN_V_DIM:ATTN_V_DIM + 1]
        o = o0 - lam * o1
        ms = jnp.mean(o * o, axis=0, keepdims=True)
        y = o * lax.rsqrt(ms + NORM_EPS) * gsub_ref[...]
        o_ref[0, h * ATTN_V_DIM:(h + 1) * ATTN_V_DIM, :] = y.astype(BF16)


def _attention(l, lq1, lk1, lq2, lk2, gsub, qt, k, vt, *, blk, big, lambda_init):
    B, S, _ = k.shape
    small = lambda shape: _layer_spec(l, shape)
    return pl.pallas_call(
        functools.partial(_attn_kernel, blk=blk, big=big, lambda_init=lambda_init),
        grid=(B, S // blk),
        in_specs=[
            small((1, ATTN_QK_DIM)), small((1, ATTN_QK_DIM)),
            small((1, ATTN_QK_DIM)), small((1, ATTN_QK_DIM)),
            small((ATTN_V_DIM, blk)),
            pl.BlockSpec((1, GROUP, blk), lambda b, t: (b, 0, t)),
            pl.BlockSpec((1, S, GROUP), lambda b, t: (b, 0, 0), pipeline_mode=pl.Buffered(1)),
            pl.BlockSpec((1, ATTN_HEADS, V_ROWS, S), lambda b, t: (b, 0, 0, 0),
                         pipeline_mode=pl.Buffered(1)),
        ],
        out_specs=pl.BlockSpec((1, GROUP, blk), lambda b, t: (b, 0, t)),
        out_shape=jax.ShapeDtypeStruct((B, GROUP, S), BF16),
        scratch_shapes=[
            pltpu.VMEM((N_COMBO, GROUP, blk), BF16),
            pltpu.VMEM((N_COMBO, 1, blk), F32),
            pltpu.VMEM((N_COMBO, V_ROWS, blk), F32),
            pltpu.VMEM((N_COMBO, big + 8, blk), F32),
            pltpu.VMEM((N_COMBO, blk, blk), F32),
            pltpu.VMEM((N_COMBO, 1, blk), F32),
        ],
        compiler_params=_params(("arbitrary", "arbitrary")),
        name="attn",
    )(lq1, lk1, lq2, lk2, gsub, qt, k, vt)


def _linear_scan(a, u, h0):
    rows, width = a.shape
    groups = rows // 8
    a3 = a.reshape(groups, 8, width)
    u3 = u.reshape(groups, 8, width)
    sub = lax.broadcasted_iota(jnp.int32, (1, 8, width), 1)
    for d in (1, 2, 4):
        keep = sub >= d
        a_prev = pltpu.roll(a3, d, 1)
        u_prev = pltpu.roll(u3, d, 1)
        u3 = jnp.where(keep, a3 * u_prev + u3, u3)
        a3 = jnp.where(keep, a3 * a_prev, a3)
    carry = h0
    out = []
    for g in range(groups):
        hg = u3[g] + a3[g] * carry
        out.append(hg)
        carry = hg[7:8, :]
    return jnp.concatenate(out, axis=0)


def _causal_taps(cur, prev_ref, w_ref, taps):
    halo = prev_ref.shape[0]
    tm = cur.shape[0]
    assert halo % 8 == 0 and halo >= taps - 1
    ext = jnp.concatenate([prev_ref[...], cur], axis=0)
    prev_ref[...] = cur[tm - halo:tm]
    acc = None
    for r in range(min(8, taps)):
        rot = ext if r == 0 else pltpu.roll(ext, r, 0)
        for a in range((taps - 1 - r) // 8 + 1):
            k = taps - 1 - (8 * a + r)
            term = w_ref[k:k + 1, :] * rot[halo - 8 * a:halo - 8 * a + tm]
            acc = term if acc is None else acc + term
    return acc


def _mixer_kernel(h_ref, rest_ref, ybt_ref, lcw_ref, lcb_ref, wgate_ref, bgate_ref, lam_ref,
                  cw_ref, cb_ref, lng_ref, lnb_ref, scw_ref, wout_ref, gpost_ref,
                  out_ref, axprev_ref, cprev_ref, dprev_ref, hstate_ref, *, tm):
    G = GROUP

    @pl.when(pl.program_id(1) == 0)
    def _():
        axprev_ref[...] = jnp.zeros(axprev_ref.shape, F32)
        cprev_ref[...] = jnp.zeros(cprev_ref.shape, F32)
        dprev_ref[...] = jnp.zeros(dprev_ref.shape, F32)
        hstate_ref[...] = jnp.zeros(hstate_ref.shape, F32)

    xa = _causal_taps(rest_ref[0, :, 0:G], axprev_ref, lcw_ref, LRU_CONV) + lcb_ref[...]
    gates = jnp.dot(xa.astype(BF16), wgate_ref[...], preferred_element_type=F32) + bgate_ref[...]
    r = jax.nn.sigmoid(gates[:, 0:G])
    ig = jax.nn.sigmoid(gates[:, G:2 * G])
    neg_lam = -lam_ref[...]
    softplus = jnp.maximum(neg_lam, 0.0) + jnp.log1p(jnp.exp(-jnp.abs(neg_lam)))
    log_a = (-LRU_C) * r * softplus
    a = jnp.exp(log_a)
    u = jnp.sqrt(1.0 - a * a) * (ig * xa)
    hh = _linear_scan(a, u, hstate_ref[0:1, :])
    hstate_ref[0:1, :] = hh[tm - 1:tm]
    ya = _gelu_tanh_times(rest_ref[0, :, G:2 * G], hh)

    c = rest_ref[0, :, 2 * G:3 * G] * jax.nn.sigmoid(rest_ref[0, :, 3 * G:4 * G])
    cc = _causal_taps(c, cprev_ref, cw_ref, CONF_KERNEL) + cb_ref[...]
    mu = jnp.mean(cc, axis=-1, keepdims=True)
    xc = cc - mu
    var = jnp.mean(xc * xc, axis=-1, keepdims=True)
    ln = xc * lax.rsqrt(var + LN_EPS) * lng_ref[...] + lnb_ref[...]
    yc = ln * jax.nn.sigmoid(ln)

    dd = rest_ref[0, :, 6 * G:7 * G] * rest_ref[0, :, 4 * G:5 * G]
    yd = rest_ref[0, :, 5 * G:6 * G] * _causal_taps(dd, dprev_ref, scw_ref, SC_KERNEL)

    m = jnp.dot(ya.astype(BF16), wout_ref[0:G, :], preferred_element_type=F32)
    m += lax.dot_general(ybt_ref[0], wout_ref[G:2 * G, :], TN_DIMS, preferred_element_type=F32)
    m += jnp.dot(yc.astype(BF16), wout_ref[2 * G:3 * G, :], preferred_element_type=F32)
    m += jnp.dot(yd.astype(BF16), wout_ref[3 * G:4 * G, :], preferred_element_type=F32)
    out_ref[0] = h_ref[0] + _rms(m, gpost_ref[...])


def _mixer(l, h, rest, ybt, lcw, lcb, wgate, bgate, lam, cw, cb, lng, lnb, scw, wout, gpost, *, tm):
    B, S, D = h.shape
    G = GROUP
    return pl.pallas_call(
        functools.partial(_mixer_kernel, tm=tm),
        grid=(B, S // tm),
        in_specs=[
            pl.BlockSpec((1, tm, D), lambda b, t: (b, t, 0)),
            pl.BlockSpec((1, tm, REST_W), lambda b, t: (b, t, 0)),
            pl.BlockSpec((1, G, tm), lambda b, t: (b, 0, t)),
            _layer_spec(l, (LRU_CONV, G)), _layer_spec(l, (1, G)),
            _layer_spec(l, (G, 2 * G)), _layer_spec(l, (1, 2 * G)), _layer_spec(l, (1, G)),
            _layer_spec(l, (CONF_KERNEL, G)), _layer_spec(l, (1, G)),
            _layer_spec(l, (1, G)), _layer_spec(l, (1, G)),
            _layer_spec(l, (SC_KERNEL, G)),
            _layer_spec(l, (D, D)), _layer_spec(l, (1, D)),
        ],
        out_specs=pl.BlockSpec((1, tm, D), lambda b, t: (b, t, 0)),
        out_shape=jax.ShapeDtypeStruct((B, S, D), F32),
        scratch_shapes=[
            pltpu.VMEM((8, G), F32),
            pltpu.VMEM((32, G), F32),
            pltpu.VMEM((8, G), F32),
            pltpu.VMEM((8, G), F32),
        ],
        compiler_params=_params(("arbitrary", "arbitrary")),
        name="mixer",
    )(h, rest, ybt, lcw, lcb, wgate, bgate, lam, cw, cb, lng, lnb, scw, wout, gpost)


def _ffn_kernel(h_ref, gpre_ref, wup_ref, cw_ref, cb_ref, wdown_ref, gpost_ref,
                out_ref, gprev_ref, *, tm, chunk):
    @pl.when(pl.program_id(1) == 0)
    def _():
        gprev_ref[...] = jnp.zeros(gprev_ref.shape, F32)

    x = h_ref[0]
    yb = _rms(x, gpre_ref[...]).astype(BF16)
    n_chunks = D_FF // chunk

    def up(c):
        lo = c * chunk
        u = jnp.dot(yb, wup_ref[:, lo:lo + chunk], preferred_element_type=F32)
        g = jnp.dot(yb, wup_ref[:, D_FF + lo:D_FF + lo + chunk], preferred_element_type=F32)
        return u, g

    f = jnp.zeros((tm, D_MODEL), F32)
    ups = [up(c) for c in range(n_chunks)]
    for c in range(n_chunks):
        lo = c * chunk
        u, g = ups[c]
        gc = _causal_taps(g, gprev_ref.at[c], cw_ref.at[:, lo:lo + chunk], FFN_KERNEL)
        gc = gc + cb_ref[:, lo:lo + chunk]
        act = (_gelu_tanh_times(gc, u)).astype(BF16)
        f += jnp.dot(act, wdown_ref[lo:lo + chunk, :], preferred_element_type=F32)
    out_ref[0] = x + _rms(f, gpost_ref[...])


def _ffn(l, h, gpre, wup, cw, cb, wdown, gpost, *, tm, chunk):
    B, S, D = h.shape
    return pl.pallas_call(
        functools.partial(_ffn_kernel, tm=tm, chunk=chunk),
        grid=(B, S // tm),
        in_specs=[
            pl.BlockSpec((1, tm, D), lambda b, t: (b, t, 0)),
            _layer_spec(l, (1, D)),
            _layer_spec(l, (D, 2 * D_FF)),
            _layer_spec(l, (FFN_KERNEL, D_FF)), _layer_spec(l, (1, D_FF)),
            _layer_spec(l, (D_FF, D)), _layer_spec(l, (1, D)),
        ],
        out_specs=pl.BlockSpec((1, tm, D), lambda b, t: (b, t, 0)),
        out_shape=jax.ShapeDtypeStruct((B, S, D), F32),
        scratch_shapes=[pltpu.VMEM((D_FF // chunk, 8, chunk), F32)],
        compiler_params=_params(("arbitrary", "arbitrary")),
        name="ffn",
    )(h, gpre, wup, cw, cb, wdown, gpost)


def _block_diag(w):
    heads, hd, _ = w.shape
    eye = jnp.eye(heads, dtype=w.dtype)
    return (eye[:, None, :, None] * w[:, :, None, :]).reshape(heads * hd, heads * hd)


def _pick_tile(S, want):
    t = min(want, S)
    while S % t:
        t //= 2
    return t


def kernel(x, norm_mix_pre, norm_mix_post, norm_ffn_pre, norm_ffn_post, w_in, lru_conv_w, lru_conv_b, lru_wa, lru_ba, lru_wx, lru_bx, lru_lambda, attn_lq1, attn_lk1, attn_lq2, attn_lk2, attn_subln, conf_dw_w, conf_dw_b, conf_ln_g, conf_ln_b, sc_conv_w, w_out, ffn_w_up, ffn_conv_w, ffn_conv_b, ffn_w_down):
    B, S, D = x.shape
    assert D == D_MODEL
    depth = w_in.shape[0]
    tm = _pick_tile(S, 512)
    blk = _pick_tile(S, 256)
    big = _pick_tile(S, 1024)
    assert tm >= 32 and blk % 128 == 0 and big % blk == 0
    layers = lambda v: v.reshape(depth, 1, -1)
    bf16 = lambda v: v.astype(BF16)

    w_in_b, w_out_b, w_up_b, w_down_b = bf16(w_in), bf16(w_out), bf16(ffn_w_up), bf16(ffn_w_down)
    wgate = bf16(jnp.concatenate([jax.vmap(_block_diag)(lru_wa), jax.vmap(_block_diag)(lru_wx)],
                                 axis=2))
    bgate = layers(jnp.concatenate([lru_ba, lru_bx], axis=1))
    lambda_init = [0.8 - 0.6 * math.exp(-0.3 * l) for l in range(depth)]
    gsub = jnp.stack([jnp.broadcast_to((attn_subln[l] * (1.0 - lambda_init[l]))[:, None],
                                       (ATTN_V_DIM, blk)) for l in range(depth)])
    g_mix_pre, g_mix_post = layers(norm_mix_pre), layers(norm_mix_post)
    g_ffn_pre, g_ffn_post = layers(norm_ffn_pre), layers(norm_ffn_post)
    lq1, lk1, lq2, lk2 = layers(attn_lq1), layers(attn_lk1), layers(attn_lq2), layers(attn_lk2)
    lcb, lam, cb = layers(lru_conv_b), layers(lru_lambda), layers(conf_dw_b)
    lng, lnb, fcb = layers(conf_ln_g), layers(conf_ln_b), layers(ffn_conv_b)

    h = x
    for l in range(depth):
        rest, k, qt, vt = _inproj(l, h, g_mix_pre, w_in_b, tm=_pick_tile(S, 1024))
        ybt = _attention(l, lq1, lk1, lq2, lk2, gsub, qt, k, vt,
                         blk=blk, big=big, lambda_init=lambda_init[l])
        h = _mixer(l, h, rest, ybt, lru_conv_w, lcb, wgate, bgate, lam, conf_dw_w, cb, lng, lnb,
                   sc_conv_w, w_out_b, g_mix_post, tm=tm)
        h = _ffn(l, h, g_ffn_pre, w_up_b, ffn_conv_w, fcb, w_down_b, g_ffn_post,
                 tm=tm, chunk=256)
    return h
```
